```python
import jax, jax.numpy as jnp
from jax import lax
import numpy as np

D_MODEL = 2048
BATCH = 4
SEQ = 4096
DEPTH = 2

N_A = DEPTH // 2
N_B = DEPTH - N_A
CHUNK = 128
D_A = D_MODEL
G_A = 16
GD_A = D_A // G_A
N_HEADS = 16
HEAD_DIM = 128
D_B = N_HEADS * HEAD_DIM
BLOCK_Q = 128
D_FF = 5632
CONV_W = 3
EPS = 1e-6

kernel_name = "yoco_gmlp_stickbreaking_hybrid"


def rms_norm(x, g):
    xf = x.astype(jnp.float32)
    ms = jnp.mean(xf * xf, axis=-1, keepdims=True)
    return (xf * lax.rsqrt(ms + EPS) * g.astype(jnp.float32)).astype(x.dtype)


def chunked_gmlp(h, w_in, v_g, w_s, b_s, w_out):
    B, S, _ = h.shape
    uv = jax.nn.gelu(h @ w_in)
    u, v = jnp.split(uv, 2, axis=-1)
    v = rms_norm(v.reshape(B, S, G_A, GD_A), v_g.reshape(G_A, GD_A))
    v = v.reshape(B, S // CHUNK, CHUNK, G_A, GD_A)
    causal = jnp.tril(jnp.ones((CHUNK, CHUNK), dtype=w_s.dtype))
    w = w_s * causal[None]
    mixed = jnp.einsum('gts,bnsgd->bntgd', w, v) + b_s.T[None, None, :, :, None]
    out = u * mixed.reshape(B, S, D_A)
    return out @ w_out


def stick_breaking_attention(q, k, v):
    B, S, H, dh = q.shape
    nb = S // BLOCK_Q
    scale = 1.0 / np.sqrt(dh).astype(np.float32)
    qb = q.reshape(B, nb, BLOCK_Q, H, dh).transpose(1, 0, 2, 3, 4)
    kf = k.astype(jnp.float32)
    vf = v.astype(jnp.float32)
    kpos = jnp.arange(S)

    def one_block(args):
        qi, i = args
        z = jnp.einsum('bqhd,bkhd->bhqk', qi.astype(jnp.float32), kf) * scale
        qpos = i * BLOCK_Q + jnp.arange(BLOCK_Q)
        mask = kpos[None, :] < qpos[:, None]
        log_beta = jax.nn.log_sigmoid(z)
        log_1m = jnp.where(mask, jax.nn.log_sigmoid(-z), 0.0)
        suffix = jnp.flip(jnp.cumsum(jnp.flip(log_1m, -1), axis=-1), -1) - log_1m
        a = jnp.where(mask, jnp.exp(log_beta + suffix), 0.0)
        return jnp.einsum('bhqk,bkhd->bqhd', a, vf)

    out = lax.map(one_block, (qb, jnp.arange(nb)))
    return out.transpose(1, 0, 2, 3, 4).reshape(B, S, H * dh).astype(q.dtype)


def conv_ffn(h, w_up, conv_w, conv_b, w_down):
    S = h.shape[1]
    a = h @ w_up
    ap = jnp.pad(a, ((0, 0), (CONV_W - 1, 0), (0, 0)))
    c = conv_b + sum(ap[:, tap:tap + S] * conv_w[tap] for tap in range(CONV_W))
    gate, val = jnp.split(c, 2, axis=-1)
    return (jax.nn.silu(gate) * val) @ w_down


def setup_inputs(seed: int = 0) -> dict:
    key = jax.random.key(seed)
    ks = jax.random.split(key, 24)
    f32 = jnp.float32

    def nrm(k, shape, scale):
        return jax.random.normal(k, shape, f32) * scale

    def gain(k, shape):
        return 1.0 + 0.02 * jax.random.normal(k, shape, f32)

    return {
        "x": jax.random.normal(ks[0], (BATCH, SEQ, D_MODEL), f32),
        "pre_mix_g": gain(ks[1], (DEPTH, D_MODEL)),
        "post_mix_g": gain(ks[2], (DEPTH, D_MODEL)),
        "pre_ffn_g": gain(ks[3], (DEPTH, D_MODEL)),
        "post_ffn_g": gain(ks[4], (DEPTH, D_MODEL)),
        "a_w_in": nrm(ks[5], (N_A, D_MODEL, 2 * D_A), D_MODEL ** -0.5),
        "a_v_norm_g": gain(ks[6], (N_A, D_A)),
        "a_w_spatial": nrm(ks[7], (N_A, G_A, CHUNK, CHUNK), CHUNK ** -0.5),
        "a_b_spatial": gain(ks[8], (N_A, G_A, CHUNK)),
        "a_w_out": nrm(ks[9], (N_A, D_A, D_MODEL), D_A ** -0.5),
        "kv_norm_g": gain(ks[10], (D_MODEL,)),
        "w_k": nrm(ks[11], (D_MODEL, D_B), D_MODEL ** -0.5),
        "w_v": nrm(ks[12], (D_MODEL, D_B), D_MODEL ** -0.5),
        "b_w_q": nrm(ks[13], (N_B, D_MODEL, D_B), D_MODEL ** -0.5),
        "b_w_o": nrm(ks[14], (N_B, D_B, D_MODEL), D_B ** -0.5),
        "ffn_w_up": nrm(ks[15], (DEPTH, D_MODEL, 2 * D_FF), D_MODEL ** -0.5),
        "ffn_conv_w": nrm(ks[16], (DEPTH, CONV_W, 2 * D_FF), CONV_W ** -0.5),
        "ffn_conv_b": nrm(ks[17], (DEPTH, 2 * D_FF), 0.02),
        "ffn_w_down": nrm(ks[18], (DEPTH, D_FF, D_MODEL), D_FF ** -0.5),
    }


def reference(x, pre_mix_g, post_mix_g, pre_ffn_g, post_ffn_g,
              a_w_in, a_v_norm_g, a_w_spatial, a_b_spatial, a_w_out,
              kv_norm_g, w_k, w_v, b_w_q, b_w_o,
              ffn_w_up, ffn_conv_w, ffn_conv_b, ffn_w_down):
    B, S, _ = x.shape
    h = x
    k_shared = None
    v_shared = None
    for layer in range(DEPTH):
        hn = rms_norm(h, pre_mix_g[layer])
        if layer < N_A:
            mix = chunked_gmlp(hn, a_w_in[layer], a_v_norm_g[layer],
                               a_w_spatial[layer], a_b_spatial[layer], a_w_out[layer])
        else:
            j = layer - N_A
            if j == 0:
                kvn = rms_norm(h, kv_norm_g)
                k_shared = (kvn @ w_k).reshape(B, S, N_HEADS, HEAD_DIM)
                v_shared = (kvn @ w_v).reshape(B, S, N_HEADS, HEAD_DIM)
            q = (hn @ b_w_q[j]).reshape(B, S, N_HEADS, HEAD_DIM)
            mix = stick_breaking_attention(q, k_shared, v_shared) @ b_w_o[j]
        h = h + rms_norm(mix, post_mix_g[layer])
        f = conv_ffn(rms_norm(h, pre_ffn_g[layer]), ffn_w_up[layer],
                     ffn_conv_w[layer], ffn_conv_b[layer], ffn_w_down[layer])
        h = h + rms_norm(f, post_ffn_g[layer])
    return h
```

```python
import functools

import jax
import jax.numpy as jnp
from jax import lax
from jax.experimental import pallas as pl
from jax.experimental.pallas import tpu as pltpu

CHUNK = 128
HEAD_DIM = 128
EPS = 1e-6

V7X_LANES = 128
V7X_SUBLANES = 8
V7X_VMEM_LIMIT_BYTES = 60000 * 1024

F32 = jnp.float32
BF16 = jnp.bfloat16


def _tiles(seq_len):
    tm = 512
    tn = 512
    tq = 256
    assert seq_len % tm == 0 and seq_len % tq == 0 and tm % CHUNK == 0
    return tm, tn, tq


def _params(n_axes):
    return pltpu.CompilerParams(
        dimension_semantics=("arbitrary",) * n_axes,
        vmem_limit_bytes=V7X_VMEM_LIMIT_BYTES,
    )


def _rms(x, g):
    ms = jnp.mean(x * x, axis=-1, keepdims=True)
    return x * lax.rsqrt(ms + EPS) * g


def _dot(a, b):
    return jnp.dot(a, b, preferred_element_type=F32)


def _gmlp_kernel(h_ref, gpre_ref, wu_ref, wv_ref, vg_ref, ws_ref, bs_ref, wo_ref,
                 gpost_ref, o_ref, xn_ref, acc_ref):
    j = pl.program_id(1)

    @pl.when(j == 0)
    def _():
        xn_ref[...] = _rms(h_ref[...], gpre_ref[...]).astype(BF16)
        acc_ref[...] = jnp.zeros_like(acc_ref)

    xn = xn_ref[...]
    u = jax.nn.gelu(_dot(xn, wu_ref[...]))
    v = jax.nn.gelu(_dot(xn, wv_ref[...]))
    tm, tn = u.shape
    row = lax.broadcasted_iota(jnp.int32, (CHUNK, CHUNK), 0)
    col = lax.broadcasted_iota(jnp.int32, (CHUNK, CHUNK), 1)
    causal = row >= col
    cols = []
    for g in range(tn // CHUNK):
        gs = slice(g * CHUNK, (g + 1) * CHUNK)
        vn = _rms(v[:, gs], vg_ref[:, gs]).astype(BF16)
        w = jnp.where(causal, ws_ref[g], 0.0).astype(BF16)
        b = bs_ref[g]
        rows = []
        for c in range(tm // CHUNK):
            cs = slice(c * CHUNK, (c + 1) * CHUNK)
            mixed = _dot(w, vn[cs, :]) + b
            rows.append(u[cs, gs] * mixed)
        cols.append(jnp.concatenate(rows, axis=0))
    gated = jnp.concatenate(cols, axis=1).astype(BF16)
    acc_ref[...] += _dot(gated, wo_ref[...])

    @pl.when(j == pl.num_programs(1) - 1)
    def _():
        o_ref[...] = h_ref[...] + _rms(acc_ref[...], gpost_ref[...])


def _gmlp_layer(h, g_pre, w_in, v_g, w_s, b_s, w_out, g_post, seq_len):
    t, d = h.shape
    d_a = w_out.shape[0]
    tm, tn, _ = _tiles(seq_len)
    nj = d_a // tn
    gpt = tn // CHUNK
    row_spec = pl.BlockSpec((tm, d), lambda m, j: (m, 0))
    vec_spec = pl.BlockSpec((1, d), lambda m, j: (0, 0))
    return pl.pallas_call(
        _gmlp_kernel,
        grid=(t // tm, nj),
        in_specs=[
            row_spec,
            vec_spec,
            pl.BlockSpec((d, tn), lambda m, j: (0, j)),
            pl.BlockSpec((d, tn), lambda m, j: (0, j + nj)),
            pl.BlockSpec((1, tn), lambda m, j: (0, j)),
            pl.BlockSpec((gpt, CHUNK, CHUNK), lambda m, j: (j, 0, 0)),
            pl.BlockSpec((gpt, CHUNK, 1), lambda m, j: (j, 0, 0)),
            pl.BlockSpec((tn, d), lambda m, j: (j, 0)),
            vec_spec,
        ],
        out_specs=row_spec,
        out_shape=jax.ShapeDtypeStruct((t, d), F32),
        scratch_shapes=[pltpu.VMEM((tm, d), BF16), pltpu.VMEM((tm, d), F32)],
        compiler_params=_params(2),
        name="gmlp_mixer",
    )(h, g_pre.reshape(1, d), w_in, w_in, v_g.reshape(1, d_a), w_s,
      b_s.reshape(b_s.shape[0], CHUNK, 1), w_out, g_post.reshape(1, d))


def _shift_rows(a, prev, s):
    r = pltpu.roll(a, s, axis=0)
    rp = pltpu.roll(prev, s, axis=0)
    sub = lax.broadcasted_iota(jnp.int32, prev.shape, 0)
    head = jnp.where(sub < s, rp, r[:V7X_SUBLANES])
    return jnp.concatenate([head, r[V7X_SUBLANES:]], axis=0)


def _causal_conv3(a, prev, cw, cb):
    return (cb + cw[2:3] * a + cw[1:2] * _shift_rows(a, prev, 1)
            + cw[0:1] * _shift_rows(a, prev, 2))


def _ffn_kernel(h_ref, gpre_ref, wg_ref, wv_ref, cwg_ref, cwv_ref, cbg_ref, cbv_ref,
                wd_ref, gpost_ref, o_ref, xn_ref, acc_ref, carry_ref, *, tiles_per_seq):
    m = pl.program_id(0)
    f = pl.program_id(1)

    @pl.when(f == 0)
    def _():
        xn_ref[...] = _rms(h_ref[...], gpre_ref[...]).astype(BF16)
        acc_ref[...] = jnp.zeros_like(acc_ref)

    @pl.when(m % tiles_per_seq == 0)
    def _():
        carry_ref[f] = jnp.zeros(carry_ref.shape[1:], F32)

    xn = xn_ref[...]
    ag = _dot(xn, wg_ref[...])
    av = _dot(xn, wv_ref[...])
    tm = ag.shape[0]
    prev_g = carry_ref[f, 0]
    prev_v = carry_ref[f, 1]
    carry_ref[f, 0] = ag[tm - V7X_SUBLANES:]
    carry_ref[f, 1] = av[tm - V7X_SUBLANES:]
    cg = _causal_conv3(ag, prev_g, cwg_ref[...], cbg_ref[...])
    cv = _causal_conv3(av, prev_v, cwv_ref[...], cbv_ref[...])
    act = (cg * jax.nn.sigmoid(cg) * cv).astype(BF16)
    acc_ref[...] += _dot(act, wd_ref[...])

    @pl.when(f == pl.num_programs(1) - 1)
    def _():
        o_ref[...] = h_ref[...] + _rms(acc_ref[...], gpost_ref[...])


def _ffn_layer(h, g_pre, w_up, conv_w, conv_b, w_down, g_post, seq_len):
    t, d = h.shape
    d_ff = w_down.shape[0]
    tm, tf, _ = _tiles(seq_len)
    nf = d_ff // tf
    assert d_ff % tf == 0
    taps = conv_w.shape[0]
    assert taps == 3
    row_spec = pl.BlockSpec((tm, d), lambda m, f: (m, 0))
    vec_spec = pl.BlockSpec((1, d), lambda m, f: (0, 0))
    conv_b2 = conv_b.reshape(1, 2 * d_ff)
    return pl.pallas_call(
        functools.partial(_ffn_kernel, tiles_per_seq=seq_len // tm),
        grid=(t // tm, nf),
        in_specs=[
            row_spec,
            vec_spec,
            pl.BlockSpec((d, tf), lambda m, f: (0, f)),
            pl.BlockSpec((d, tf), lambda m, f: (0, f + nf)),
            pl.BlockSpec((taps, tf), lambda m, f: (0, f)),
            pl.BlockSpec((taps, tf), lambda m, f: (0, f + nf)),
            pl.BlockSpec((1, tf), lambda m, f: (0, f)),
            pl.BlockSpec((1, tf), lambda m, f: (0, f + nf)),
            pl.BlockSpec((tf, d), lambda m, f: (f, 0)),
            vec_spec,
        ],
        out_specs=row_spec,
        out_shape=jax.ShapeDtypeStruct((t, d), F32),
        scratch_shapes=[
            pltpu.VMEM((tm, d), BF16),
            pltpu.VMEM((tm, d), F32),
            pltpu.VMEM((nf, 2, V7X_SUBLANES, tf), F32),
        ],
        compiler_params=_params(2),
        name="conv_ffn",
    )(h, g_pre.reshape(1, d), w_up, w_up, conv_w, conv_w, conv_b2, conv_b2, w_down,
      g_post.reshape(1, d))


def _qkv_kernel(h_ref, gq_ref, gkv_ref, wq_ref, wk_ref, wv_ref, q_ref, k_ref, vt_ref,
                xq_ref, xkv_ref):
    j = pl.program_id(1)

    @pl.when(j == 0)
    def _():
        x = h_ref[...]
        xs = x * lax.rsqrt(jnp.mean(x * x, axis=-1, keepdims=True) + EPS)
        xq_ref[...] = (xs * gq_ref[...]).astype(BF16)
        xkv_ref[...] = (xs * gkv_ref[...]).astype(BF16)

    scale = HEAD_DIM ** -0.5
    q = _dot(xq_ref[...], wq_ref[...]) * scale
    k = _dot(xkv_ref[...], wk_ref[...])
    v = _dot(xkv_ref[...], wv_ref[...])
    for hh in range(q.shape[1] // HEAD_DIM):
        hs = slice(hh * HEAD_DIM, (hh + 1) * HEAD_DIM)
        q_ref[hh] = q[:, hs].astype(BF16)
        k_ref[hh] = k[:, hs].astype(BF16)
        vt_ref[hh] = v[:, hs].T.astype(BF16)


def _qkv(h, g_q, g_kv, w_q, w_k, w_v, seq_len):
    t, d = h.shape
    d_b = w_q.shape[1]
    n_heads = d_b // HEAD_DIM
    tm, tn, _ = _tiles(seq_len)
    hpt = tn // HEAD_DIM
    w_spec = pl.BlockSpec((d, tn), lambda m, j: (0, j))
    vec_spec = pl.BlockSpec((1, d), lambda m, j: (0, 0))
    hd_spec = pl.BlockSpec((hpt, tm, HEAD_DIM), lambda m, j: (j, m, 0))
    return pl.pallas_call(
        _qkv_kernel,
        grid=(t // tm, d_b // tn),
        in_specs=[pl.BlockSpec((tm, d), lambda m, j: (m, 0)), vec_spec, vec_spec,
                  w_spec, w_spec, w_spec],
        out_specs=[hd_spec, hd_spec,
                   pl.BlockSpec((hpt, HEAD_DIM, tm), lambda m, j: (j, 0, m))],
        out_shape=[jax.ShapeDtypeStruct((n_heads, t, HEAD_DIM), BF16),
                   jax.ShapeDtypeStruct((n_heads, t, HEAD_DIM), BF16),
                   jax.ShapeDtypeStruct((n_heads, HEAD_DIM, t), BF16)],
        scratch_shapes=[pltpu.VMEM((tm, d), BF16), pltpu.VMEM((tm, d), BF16)],
        compiler_params=_params(2),
        name="qkv_proj",
    )(h, g_q.reshape(1, d), g_kv.reshape(1, d), w_q, w_k, w_v)


def _attn_kernel(q_ref, k_ref, vt_ref, o_ref, *, tq):
    i = pl.program_id(2)
    q = q_ref[0]
    key = lax.broadcasted_iota(jnp.int32, (tq, tq), 0)
    oth = lax.broadcasted_iota(jnp.int32, (tq, tq), 1)
    later = (oth > key).astype(BF16)
    before_query = key < oth

    def block(jb, carry, acc, diagonal):
        start = pl.multiple_of(jb * tq, tq)
        ks = k_ref[0, pl.ds(start, tq), :]
        z = lax.dot_general(ks, q, (((1,), (1,)), ((), ())),
                            preferred_element_type=F32)
        sp = jnp.maximum(z, 0.0) + jnp.log(1.0 + jnp.exp(-jnp.abs(z)))
        log_beta = z - sp
        if diagonal:
            sp = jnp.where(before_query, sp, 0.0)
        sp_hi = sp.astype(BF16)
        sp_lo = (sp - sp_hi.astype(F32)).astype(BF16)
        suffix = _dot(later, sp_hi) + _dot(later, sp_lo)
        a = jnp.exp(log_beta - suffix + carry)
        if diagonal:
            a = jnp.where(before_query, a, 0.0)
        vt = vt_ref[0, :, pl.ds(start, tq)]
        acc = acc + _dot(vt, a.astype(BF16))
        carry = carry - jnp.sum(sp, axis=0, keepdims=True)
        return carry, acc

    carry = jnp.zeros((1, tq), F32)
    acc = jnp.zeros((HEAD_DIM, tq), F32)
    carry, acc = block(i, carry, acc, True)

    def body(n, state):
        return block(i - 1 - n, state[0], state[1], False)

    carry, acc = lax.fori_loop(0, i, body, (carry, acc))
    o_ref[...] = acc.T.astype(o_ref.dtype)


def _attention(q, k, vt, batch, seq_len):
    n_heads, t, dh = q.shape
    _, _, tq = _tiles(seq_len)
    nq = seq_len // tq
    return pl.pallas_call(
        functools.partial(_attn_kernel, tq=tq),
        grid=(batch, n_heads, nq),
        in_specs=[
            pl.BlockSpec((1, tq, dh), lambda b, h, i: (h, b * nq + i, 0)),
            pl.BlockSpec((1, seq_len, dh), lambda b, h, i: (h, b, 0)),
            pl.BlockSpec((1, dh, seq_len), lambda b, h, i: (h, 0, b)),
        ],
        out_specs=pl.BlockSpec((tq, dh), lambda b, h, i: (b * nq + i, h)),
        out_shape=jax.ShapeDtypeStruct((t, n_heads * dh), BF16),
        compiler_params=_params(3),
        name="stickbreak_attn",
    )(q, k, vt)


def _proj_kernel(a_ref, w_ref, h_ref, g_ref, o_ref):
    o_ref[...] = h_ref[...] + _rms(_dot(a_ref[...], w_ref[...]), g_ref[...])


def _proj_residual(a, w, h, g, seq_len):
    t, d = h.shape
    d_in = a.shape[1]
    tm, _, _ = _tiles(seq_len)
    return pl.pallas_call(
        _proj_kernel,
        grid=(t // tm,),
        in_specs=[pl.BlockSpec((tm, d_in), lambda m: (m, 0)),
                  pl.BlockSpec((d_in, d), lambda m: (0, 0)),
                  pl.BlockSpec((tm, d), lambda m: (m, 0)),
                  pl.BlockSpec((1, d), lambda m: (0, 0))],
        out_specs=pl.BlockSpec((tm, d), lambda m: (m, 0)),
        out_shape=jax.ShapeDtypeStruct((t, d), F32),
        compiler_params=_params(1),
        name="attn_out_proj",
    )(a, w, h, g.reshape(1, d))


def kernel(x, pre_mix_g, post_mix_g, pre_ffn_g, post_ffn_g, a_w_in, a_v_norm_g, a_w_spatial, a_b_spatial, a_w_out, kv_norm_g, w_k, w_v, b_w_q, b_w_o, ffn_w_up, ffn_conv_w, ffn_conv_b, ffn_w_down):
    batch, seq_len, d = x.shape
    n_a = a_w_in.shape[0]
    n_b = b_w_q.shape[0]
    assert n_a == 1 and n_b == 1 and pre_mix_g.shape[0] == n_a + n_b
    assert a_w_spatial.shape[2] == CHUNK and w_k.shape[1] % HEAD_DIM == 0
    h = x.reshape(batch * seq_len, d)

    h = _gmlp_layer(h, pre_mix_g[0], a_w_in[0].astype(BF16), a_v_norm_g[0], a_w_spatial[0],
                    a_b_spatial[0], a_w_out[0].astype(BF16), post_mix_g[0], seq_len)
    h = _ffn_layer(h, pre_ffn_g[0], ffn_w_up[0].astype(BF16), ffn_conv_w[0], ffn_conv_b[0],
                   ffn_w_down[0].astype(BF16), post_ffn_g[0], seq_len)

    q, k, vt = _qkv(h, pre_mix_g[1], kv_norm_g, b_w_q[0].astype(BF16), w_k.astype(BF16),
                    w_v.astype(BF16), seq_len)
    att = _attention(q, k, vt, batch, seq_len)
    h = _proj_residual(att, b_w_o[0].astype(BF16), h, post_mix_g[1], seq_len)
    h = _ffn_layer(h, pre_ffn_g[1], ffn_w_up[1].astype(BF16), ffn_conv_w[1], ffn_conv_b[1],
                   ffn_w_down[1].astype(BF16), post_ffn_g[1], seq_len)
    return h.reshape(batch, seq_len, d)
```

```python
import functools

import jax
import jax.numpy as jnp
from jax import lax
from jax.experimental import pallas as pl
from jax.experimental.pallas import tpu as pltpu

CHUNK = 128
HEAD_DIM = 128
EPS = 1e-6
LOG2_E = 1.4426950408889634
ATTN_HEADS_PER_STEP = 4
ATTN_LOG2_UNDERFLOW = 160.0

V7X_LANES = 128
V7X_SUBLANES = 8
V7X_VMEM_LIMIT_BYTES = 60000 * 1024

F32 = jnp.float32
BF16 = jnp.bfloat16


def _tiles(seq_len):
    tm = 512
    tn = 512
    tq = 256
    assert seq_len % tm == 0 and seq_len % tq == 0 and tm % CHUNK == 0
    return tm, tn, tq


def _params(n_axes):
    return pltpu.CompilerParams(
        dimension_semantics=("arbitrary",) * n_axes,
        vmem_limit_bytes=V7X_VMEM_LIMIT_BYTES,
    )


def _rms(x, g):
    ms = jnp.mean(x * x, axis=-1, keepdims=True)
    return x * lax.rsqrt(ms + EPS) * g


def _dot(a, b):
    return jnp.dot(a, b, preferred_element_type=F32)


def _gmlp_kernel(h_ref, gpre_ref, wu_ref, wv_ref, vg_ref, ws_ref, bs_ref, wo_ref,
                 gpost_ref, o_ref, xn_ref, acc_ref):
    j = pl.program_id(1)

    @pl.when(j == 0)
    def _():
        xn_ref[...] = _rms(h_ref[...], gpre_ref[...]).astype(BF16)
        acc_ref[...] = jnp.zeros_like(acc_ref)

    xn = xn_ref[...]
    u = jax.nn.gelu(_dot(xn, wu_ref[...]))
    v = jax.nn.gelu(_dot(xn, wv_ref[...]))
    tm, tn = u.shape
    row = lax.broadcasted_iota(jnp.int32, (CHUNK, CHUNK), 0)
    col = lax.broadcasted_iota(jnp.int32, (CHUNK, CHUNK), 1)
    causal = row >= col
    cols = []
    for g in range(tn // CHUNK):
        gs = slice(g * CHUNK, (g + 1) * CHUNK)
        vn = _rms(v[:, gs], vg_ref[:, gs]).astype(BF16)
        w = jnp.where(causal, ws_ref[g], 0.0).astype(BF16)
        b = bs_ref[g]
        rows = []
        for c in range(tm // CHUNK):
            cs = slice(c * CHUNK, (c + 1) * CHUNK)
            mixed = _dot(w, vn[cs, :]) + b
            rows.append(u[cs, gs] * mixed)
        cols.append(jnp.concatenate(rows, axis=0))
    gated = jnp.concatenate(cols, axis=1).astype(BF16)
    acc_ref[...] += _dot(gated, wo_ref[...])

    @pl.when(j == pl.num_programs(1) - 1)
    def _():
        o_ref[...] = h_ref[...] + _rms(acc_ref[...], gpost_ref[...])


def _gmlp_layer(h, g_pre, w_in, v_g, w_s, b_s, w_out, g_post, seq_len):
    t, d = h.shape
    d_a = w_out.shape[0]
    tm, tn, _ = _tiles(seq_len)
    nj = d_a // tn
    gpt = tn // CHUNK
    row_spec = pl.BlockSpec((tm, d), lambda m, j: (m, 0))
    vec_spec = pl.BlockSpec((1, d), lambda m, j: (0, 0))
    return pl.pallas_call(
        _gmlp_kernel,
        grid=(t // tm, nj),
        in_specs=[
            row_spec,
            vec_spec,
            pl.BlockSpec((d, tn), lambda m, j: (0, j)),
            pl.BlockSpec((d, tn), lambda m, j: (0, j + nj)),
            pl.BlockSpec((1, tn), lambda m, j: (0, j)),
            pl.BlockSpec((gpt, CHUNK, CHUNK), lambda m, j: (j, 0, 0)),
            pl.BlockSpec((gpt, CHUNK, 1), lambda m, j: (j, 0, 0)),
            pl.BlockSpec((tn, d), lambda m, j: (j, 0)),
            vec_spec,
        ],
        out_specs=row_spec,
        out_shape=jax.ShapeDtypeStruct((t, d), F32),
        scratch_shapes=[pltpu.VMEM((tm, d), BF16), pltpu.VMEM((tm, d), F32)],
        compiler_params=_params(2),
        name="gmlp_mixer",
    )(h, g_pre.reshape(1, d), w_in, w_in, v_g.reshape(1, d_a), w_s,
      b_s.reshape(b_s.shape[0], CHUNK, 1), w_out, g_post.reshape(1, d))


def _shift_rows(a, prev, s):
    r = pltpu.roll(a, s, axis=0)
    rp = pltpu.roll(prev, s, axis=0)
    sub = lax.broadcasted_iota(jnp.int32, prev.shape, 0)
    head = jnp.where(sub < s, rp, r[:V7X_SUBLANES])
    return jnp.concatenate([head, r[V7X_SUBLANES:]], axis=0)


def _causal_conv3(a, prev, cw, cb):
    return (cb + cw[2:3] * a + cw[1:2] * _shift_rows(a, prev, 1)
            + cw[0:1] * _shift_rows(a, prev, 2))


def _ffn_kernel(h_ref, gpre_ref, wg_ref, wv_ref, cwg_ref, cwv_ref, cbg_ref, cbv_ref,
                wd_ref, gpost_ref, o_ref, xn_ref, acc_ref, carry_ref, *, tiles_per_seq):
    m = pl.program_id(0)
    f = pl.program_id(1)

    @pl.when(f == 0)
    def _():
        xn_ref[...] = _rms(h_ref[...], gpre_ref[...]).astype(BF16)
        acc_ref[...] = jnp.zeros_like(acc_ref)

    @pl.when(m % tiles_per_seq == 0)
    def _():
        carry_ref[f] = jnp.zeros(carry_ref.shape[1:], F32)

    xn = xn_ref[...]
    ag = _dot(xn, wg_ref[...])
    av = _dot(xn, wv_ref[...])
    tm = ag.shape[0]
    prev_g = carry_ref[f, 0]
    prev_v = carry_ref[f, 1]
    carry_ref[f, 0] = ag[tm - V7X_SUBLANES:]
    carry_ref[f, 1] = av[tm - V7X_SUBLANES:]
    cg = _causal_conv3(ag, prev_g, cwg_ref[...], cbg_ref[...])
    cv = _causal_conv3(av, prev_v, cwv_ref[...], cbv_ref[...])
    act = (cg * jax.nn.sigmoid(cg) * cv).astype(BF16)
    acc_ref[...] += _dot(act, wd_ref[...])

    @pl.when(f == pl.num_programs(1) - 1)
    def _():
        o_ref[...] = h_ref[...] + _rms(acc_ref[...], gpost_ref[...])


def _ffn_layer(h, g_pre, w_up, conv_w, conv_b, w_down, g_post, seq_len):
    t, d = h.shape
    d_ff = w_down.shape[0]
    tm, tf, _ = _tiles(seq_len)
    nf = d_ff // tf
    assert d_ff % tf == 0
    taps = conv_w.shape[0]
    assert taps == 3
    row_spec = pl.BlockSpec((tm, d), lambda m, f: (m, 0))
    vec_spec = pl.BlockSpec((1, d), lambda m, f: (0, 0))
    conv_b2 = conv_b.reshape(1, 2 * d_ff)
    return pl.pallas_call(
        functools.partial(_ffn_kernel, tiles_per_seq=seq_len // tm),
        grid=(t // tm, nf),
        in_specs=[
            row_spec,
            vec_spec,
            pl.BlockSpec((d, tf), lambda m, f: (0, f)),
            pl.BlockSpec((d, tf), lambda m, f: (0, f + nf)),
            pl.BlockSpec((taps, tf), lambda m, f: (0, f)),
            pl.BlockSpec((taps, tf), lambda m, f: (0, f + nf)),
            pl.BlockSpec((1, tf), lambda m, f: (0, f)),
            pl.BlockSpec((1, tf), lambda m, f: (0, f + nf)),
            pl.BlockSpec((tf, d), lambda m, f: (f, 0)),
            vec_spec,
        ],
        out_specs=row_spec,
        out_shape=jax.ShapeDtypeStruct((t, d), F32),
        scratch_shapes=[
            pltpu.VMEM((tm, d), BF16),
            pltpu.VMEM((tm, d), F32),
            pltpu.VMEM((nf, 2, V7X_SUBLANES, tf), F32),
        ],
        compiler_params=_params(2),
        name="conv_ffn",
    )(h, g_pre.reshape(1, d), w_up, w_up, conv_w, conv_w, conv_b2, conv_b2, w_down,
      g_post.reshape(1, d))


def _qkv_kernel(h_ref, gq_ref, gkv_ref, wq_ref, wk_ref, wv_ref, q_ref, k_ref, vt_ref,
                xq_ref, xkv_ref):
    j = pl.program_id(1)

    @pl.when(j == 0)
    def _():
        x = h_ref[...]
        xs = x * lax.rsqrt(jnp.mean(x * x, axis=-1, keepdims=True) + EPS)
        xq_ref[...] = (xs * gq_ref[...]).astype(BF16)
        xkv_ref[...] = (xs * gkv_ref[...]).astype(BF16)

    scale = LOG2_E * HEAD_DIM ** -0.5
    q = _dot(xq_ref[...], wq_ref[...]) * scale
    k = _dot(xkv_ref[...], wk_ref[...])
    v = _dot(xkv_ref[...], wv_ref[...])
    for hh in range(q.shape[1] // HEAD_DIM):
        hs = slice(hh * HEAD_DIM, (hh + 1) * HEAD_DIM)
        q_ref[hh] = q[:, hs].astype(BF16)
        k_ref[hh] = k[:, hs].astype(BF16)
        vt_ref[hh] = v[:, hs].T.astype(BF16)


def _qkv(h, g_q, g_kv, w_q, w_k, w_v, seq_len):
    t, d = h.shape
    d_b = w_q.shape[1]
    n_heads = d_b // HEAD_DIM
    tm, tn, _ = _tiles(seq_len)
    hpt = tn // HEAD_DIM
    w_spec = pl.BlockSpec((d, tn), lambda m, j: (0, j))
    vec_spec = pl.BlockSpec((1, d), lambda m, j: (0, 0))
    hd_spec = pl.BlockSpec((hpt, tm, HEAD_DIM), lambda m, j: (j, m, 0))
    return pl.pallas_call(
        _qkv_kernel,
        grid=(t // tm, d_b // tn),
        in_specs=[pl.BlockSpec((tm, d), lambda m, j: (m, 0)), vec_spec, vec_spec,
                  w_spec, w_spec, w_spec],
        out_specs=[hd_spec, hd_spec,
                   pl.BlockSpec((hpt, HEAD_DIM, tm), lambda m, j: (j, 0, m))],
        out_shape=[jax.ShapeDtypeStruct((n_heads, t, HEAD_DIM), BF16),
                   jax.ShapeDtypeStruct((n_heads, t, HEAD_DIM), BF16),
                   jax.ShapeDtypeStruct((n_heads, HEAD_DIM, t), BF16)],
        scratch_shapes=[pltpu.VMEM((tm, d), BF16), pltpu.VMEM((tm, d), BF16)],
        compiler_params=_params(2),
        name="qkv_proj",
    )(h, g_q.reshape(1, d), g_kv.reshape(1, d), w_q, w_k, w_v)


def _attn_kernel(q_ref, k_ref, vt_ref, later_ref, o_ref, acc_ref, *, tq, hp):
    i = pl.program_id(2)
    key = lax.broadcasted_iota(jnp.int32, (tq, tq), 0)
    qry = lax.broadcasted_iota(jnp.int32, (tq, tq), 1)
    before_query = key < qry

    def chain(hh, jb, carry, diagonal, first=False, keep=None):
        start = pl.multiple_of(jb * tq, tq)
        ks = k_ref[hh, pl.ds(start, tq), :]
        z = lax.dot_general(ks, q_ref[hh], (((1,), (1,)), ((), ())),
                            preferred_element_type=F32)
        m = jnp.maximum(z, 0.0)
        zm = z - m
        l = jnp.log(1.0 + jnp.exp2(zm - m)) * LOG2_E
        sp = m + l
        log_beta = zm - l
        if diagonal:
            sp = jnp.where(before_query, sp, 0.0)
        sp_hi = sp.astype(BF16)
        sp_lo = (sp - sp_hi.astype(F32)).astype(BF16)
        suffix = _dot(later_ref[...], jnp.concatenate([sp_hi, sp_lo], axis=0))
        a = jnp.exp2(log_beta - suffix + carry)
        if diagonal:
            a = jnp.where(before_query, a, 0.0)
        vt = vt_ref[hh, :, pl.ds(start, tq)]
        if keep is not None:
            vt = jnp.where(keep, vt, jnp.zeros_like(vt))
        pv = _dot(vt, a.astype(BF16))
        if first:
            acc_ref[hh] = pv
        else:
            acc_ref[hh] += pv
        return carry - (suffix[0:1] + sp[0:1])

    carries = [chain(hh, i, jnp.zeros((1, tq), F32), True, first=True) for hh in range(hp)]
    carries = [chain(hh, jnp.maximum(i - 1, 0), carries[hh], False, keep=i > 0)
               for hh in range(hp)]

    least_used = functools.reduce(jnp.maximum, [jnp.max(c) for c in carries])
    remaining = jnp.where(least_used > -ATTN_LOG2_UNDERFLOW, i - 1, 0)

    def body(n, cs):
        return tuple(chain(hh, i - 2 - n, cs[hh], False) for hh in range(hp))

    lax.fori_loop(0, remaining, body, tuple(carries))
    o_ref[...] = jnp.concatenate([acc_ref[hh].T for hh in range(hp)], axis=1).astype(o_ref.dtype)


def _attention(q, k, vt, batch, seq_len):
    n_heads, t, dh = q.shape
    _, _, tq = _tiles(seq_len)
    hp = ATTN_HEADS_PER_STEP
    assert n_heads % hp == 0
    nq = seq_len // tq
    tri = jnp.triu(jnp.ones((tq, tq), BF16), k=1)
    later = jnp.concatenate([tri, tri], axis=1)
    return pl.pallas_call(
        functools.partial(_attn_kernel, tq=tq, hp=hp),
        grid=(batch, n_heads // hp, nq),
        in_specs=[
            pl.BlockSpec((hp, tq, dh), lambda b, g, i: (g, b * nq + i, 0)),
            pl.BlockSpec((hp, seq_len, dh), lambda b, g, i: (g, b, 0)),
            pl.BlockSpec((hp, dh, seq_len), lambda b, g, i: (g, 0, b)),
            pl.BlockSpec((tq, 2 * tq), lambda b, g, i: (0, 0)),
        ],
        out_specs=pl.BlockSpec((tq, hp * dh), lambda b, g, i: (b * nq + i, g)),
        out_shape=jax.ShapeDtypeStruct((t, n_heads * dh), BF16),
        scratch_shapes=[pltpu.VMEM((hp, dh, tq), F32)],
        compiler_params=_params(3),
        name="stickbreak_attn",
    )(q, k, vt, later)


def _proj_kernel(a_ref, w_ref, h_ref, g_ref, o_ref):
    o_ref[...] = h_ref[...] + _rms(_dot(a_ref[...], w_ref[...]), g_ref[...])


def _proj_residual(a, w, h, g, seq_len):
    t, d = h.shape
    d_in = a.shape[1]
    tm, _, _ = _tiles(seq_len)
    return pl.pallas_call(
        _proj_kernel,
        grid=(t // tm,),
        in_specs=[pl.BlockSpec((tm, d_in), lambda m: (m, 0)),
                  pl.BlockSpec((d_in, d), lambda m: (0, 0)),
                  pl.BlockSpec((tm, d), lambda m: (m, 0)),
                  pl.BlockSpec((1, d), lambda m: (0, 0))],
        out_specs=pl.BlockSpec((tm, d), lambda m: (m, 0)),
        out_shape=jax.ShapeDtypeStruct((t, d), F32),
        compiler_params=_params(1),
        name="attn_out_proj",
    )(a, w, h, g.reshape(1, d))


def kernel(x, pre_mix_g, post_mix_g, pre_ffn_g, post_ffn_g, a_w_in, a_v_norm_g, a_w_spatial, a_b_spatial, a_w_out, kv_norm_g, w_k, w_v, b_w_q, b_w_o, ffn_w_up, ffn_conv_w, ffn_conv_b, ffn_w_down):
    batch, seq_len, d = x.shape
    n_a = a_w_in.shape[0]
    n_b = b_w_q.shape[0]
    assert n_a == 1 and n_b == 1 and pre_mix_g.shape[0] == n_a + n_b
    assert a_w_spatial.shape[2] == CHUNK and w_k.shape[1] % HEAD_DIM == 0
    h = x.reshape(batch * seq_len, d)

    h = _gmlp_layer(h, pre_mix_g[0], a_w_in[0].astype(BF16), a_v_norm_g[0], a_w_spatial[0],
                    a_b_spatial[0], a_w_out[0].astype(BF16), post_mix_g[0], seq_len)
    h = _ffn_layer(h, pre_ffn_g[0], ffn_w_up[0].astype(BF16), ffn_conv_w[0], ffn_conv_b[0],
                   ffn_w_down[0].astype(BF16), post_ffn_g[0], seq_len)

    q, k, vt = _qkv(h, pre_mix_g[1], kv_norm_g, b_w_q[0].astype(BF16), w_k.astype(BF16),
                    w_v.astype(BF16), seq_len)
    att = _attention(q, k, vt, batch, seq_len)
    h = _proj_residual(att, b_w_o[0].astype(BF16), h, post_mix_g[1], seq_len)
    h = _ffn_layer(h, pre_ffn_g[1], ffn_w_up[1].astype(BF16), ffn_conv_w[1], ffn_conv_b[1],
                   ffn_w_down[1].astype(BF16), post_ffn_g[1], seq_len)
    return h.reshape(batch, seq_len, d)
```

```python
import functools
from typing import NamedTuple

import jax
import jax.numpy as jnp
from jax import lax
from jax.experimental import pallas as pl
from jax.experimental.pallas import tpu as pltpu

CHUNK = 128
HEAD_DIM = 128
EPS = 1e-6
LOG2_E = 1.4426950408889634
ATTN_HEADS_PER_STEP = 8
ATTN_LOG2_UNDERFLOW = 160.0

V7X_LANES = 128
V7X_SUBLANES = 8
V7X_VMEM_LIMIT_BYTES = 60000 * 1024

F32 = jnp.float32
BF16 = jnp.bfloat16


class _Tiles(NamedTuple):
    rows: int
    ffn_rows: int
    cols: int
    attn: int


def _tiles(seq_len):
    tiles = _Tiles(rows=1024, ffn_rows=1024, cols=512, attn=256)
    assert all(seq_len % n == 0 for n in (tiles.rows, tiles.ffn_rows, tiles.attn))
    assert tiles.rows % CHUNK == 0
    return tiles


def _params(n_axes):
    return pltpu.CompilerParams(
        dimension_semantics=("arbitrary",) * n_axes,
        vmem_limit_bytes=V7X_VMEM_LIMIT_BYTES,
    )


def _rms(x, g):
    ms = jnp.mean(x * x, axis=-1, keepdims=True)
    return x * lax.rsqrt(ms + EPS) * g


def _dot(a, b):
    return jnp.dot(a, b, preferred_element_type=F32)


def _gmlp_kernel(h_ref, gpre_ref, wu_ref, wv_ref, vg_ref, ws_ref, bs_ref, wo_ref,
                 gpost_ref, o_ref, xn_ref):
    j = pl.program_id(1)

    @pl.when(j == 0)
    def _():
        xn_ref[...] = _rms(h_ref[...], gpre_ref[...]).astype(BF16)
        o_ref[...] = jnp.zeros_like(o_ref)

    xn = xn_ref[...]
    u = jax.nn.gelu(_dot(xn, wu_ref[...]))
    v = jax.nn.gelu(_dot(xn, wv_ref[...]))
    tm, tn = u.shape
    row = lax.broadcasted_iota(jnp.int32, (CHUNK, CHUNK), 0)
    col = lax.broadcasted_iota(jnp.int32, (CHUNK, CHUNK), 1)
    causal = row >= col
    cols = []
    for g in range(tn // CHUNK):
        gs = slice(g * CHUNK, (g + 1) * CHUNK)
        vn = _rms(v[:, gs], vg_ref[:, gs]).astype(BF16)
        w = jnp.where(causal, ws_ref[g], 0.0).astype(BF16)
        b = bs_ref[g]
        rows = []
        for c in range(tm // CHUNK):
            cs = slice(c * CHUNK, (c + 1) * CHUNK)
            mixed = _dot(w, vn[cs, :]) + b
            rows.append(u[cs, gs] * mixed)
        cols.append(jnp.concatenate(rows, axis=0))
    gated = jnp.concatenate(cols, axis=1).astype(BF16)
    o_ref[...] += _dot(gated, wo_ref[...])

    @pl.when(j == pl.num_programs(1) - 1)
    def _():
        o_ref[...] = h_ref[...] + _rms(o_ref[...], gpost_ref[...])


def _gmlp_layer(h, g_pre, w_in, v_g, w_s, b_s, w_out, g_post, seq_len):
    t, d = h.shape
    d_a = w_out.shape[0]
    tm, tn = _tiles(seq_len).rows, _tiles(seq_len).cols
    nj = d_a // tn
    gpt = tn // CHUNK
    row = lambda m, j: (m, 0)
    vec_spec = pl.BlockSpec((1, d), lambda m, j: (0, 0))
    return pl.pallas_call(
        _gmlp_kernel,
        grid=(t // tm, nj),
        in_specs=[
            pl.BlockSpec((tm, d), row, pipeline_mode=pl.Buffered(1)),
            vec_spec,
            pl.BlockSpec((d, tn), lambda m, j: (0, j)),
            pl.BlockSpec((d, tn), lambda m, j: (0, j + nj)),
            pl.BlockSpec((1, tn), lambda m, j: (0, j)),
            pl.BlockSpec((gpt, CHUNK, CHUNK), lambda m, j: (j, 0, 0)),
            pl.BlockSpec((gpt, CHUNK, 1), lambda m, j: (j, 0, 0)),
            pl.BlockSpec((tn, d), lambda m, j: (j, 0)),
            vec_spec,
        ],
        out_specs=pl.BlockSpec((tm, d), row),
        out_shape=jax.ShapeDtypeStruct((t, d), F32),
        scratch_shapes=[pltpu.VMEM((tm, d), BF16)],
        compiler_params=_params(2),
        name="gmlp_mixer",
    )(h, g_pre.reshape(1, d), w_in, w_in, v_g.reshape(1, d_a), w_s,
      b_s.reshape(b_s.shape[0], CHUNK, 1), w_out, g_post.reshape(1, d))


def _shift_rows(a, prev, s):
    r = pltpu.roll(a, s, axis=0)
    rp = pltpu.roll(prev, s, axis=0)
    sub = lax.broadcasted_iota(jnp.int32, prev.shape, 0)
    head = jnp.where(sub < s, rp, r[:V7X_SUBLANES])
    return jnp.concatenate([head, r[V7X_SUBLANES:]], axis=0)


def _causal_conv3(a, prev, cw, cb):
    return (cb + cw[2:3] * a + cw[1:2] * _shift_rows(a, prev, 1)
            + cw[0:1] * _shift_rows(a, prev, 2))


def _ffn_kernel(h_ref, gpre_ref, wg_ref, wv_ref, cwg_ref, cwv_ref, cbg_ref, cbv_ref,
                wd_ref, gpost_ref, o_ref, xn_ref, carry_ref, *, tiles_per_seq):
    m = pl.program_id(0)
    f = pl.program_id(1)

    @pl.when(f == 0)
    def _():
        xn_ref[...] = _rms(h_ref[...], gpre_ref[...]).astype(BF16)
        o_ref[...] = jnp.zeros_like(o_ref)

    @pl.when(m % tiles_per_seq == 0)
    def _():
        carry_ref[f] = jnp.zeros(carry_ref.shape[1:], F32)

    xn = xn_ref[...]
    ag = _dot(xn, wg_ref[...])
    av = _dot(xn, wv_ref[...])
    tm = ag.shape[0]
    prev_g = carry_ref[f, 0]
    prev_v = carry_ref[f, 1]
    carry_ref[f, 0] = ag[tm - V7X_SUBLANES:]
    carry_ref[f, 1] = av[tm - V7X_SUBLANES:]
    cg = _causal_conv3(ag, prev_g, cwg_ref[...], cbg_ref[...])
    cv = _causal_conv3(av, prev_v, cwv_ref[...], cbv_ref[...])
    act = (cg * jax.nn.sigmoid(cg) * cv).astype(BF16)
    o_ref[...] += _dot(act, wd_ref[...])

    @pl.when(f == pl.num_programs(1) - 1)
    def _():
        o_ref[...] = h_ref[...] + _rms(o_ref[...], gpost_ref[...])


def _ffn_layer(h, layer, g_pre, w_up, conv_w, conv_b, w_down, g_post, seq_len):
    t, d = h.shape
    depth, d_ff, _ = w_down.shape
    tm, tf = _tiles(seq_len).ffn_rows, _tiles(seq_len).cols
    nf = d_ff // tf
    assert d_ff % tf == 0 and conv_w.shape[1] == 3
    taps = conv_w.shape[1]
    row = lambda m, f: (m, 0)
    vec_spec = pl.BlockSpec((None, 1, d), lambda m, f: (layer, 0, 0))
    conv_b3 = conv_b.reshape(depth, 1, 2 * d_ff)
    return pl.pallas_call(
        functools.partial(_ffn_kernel, tiles_per_seq=seq_len // tm),
        grid=(t // tm, nf),
        in_specs=[
            pl.BlockSpec((tm, d), row, pipeline_mode=pl.Buffered(1)),
            vec_spec,
            pl.BlockSpec((None, d, tf), lambda m, f: (layer, 0, f)),
            pl.BlockSpec((None, d, tf), lambda m, f: (layer, 0, f + nf)),
            pl.BlockSpec((None, taps, tf), lambda m, f: (layer, 0, f)),
            pl.BlockSpec((None, taps, tf), lambda m, f: (layer, 0, f + nf)),
            pl.BlockSpec((None, 1, tf), lambda m, f: (layer, 0, f)),
            pl.BlockSpec((None, 1, tf), lambda m, f: (layer, 0, f + nf)),
            pl.BlockSpec((None, tf, d), lambda m, f: (layer, f, 0)),
            vec_spec,
        ],
        out_specs=pl.BlockSpec((tm, d), row),
        out_shape=jax.ShapeDtypeStruct((t, d), F32),
        scratch_shapes=[
            pltpu.VMEM((tm, d), BF16),
            pltpu.VMEM((nf, 2, V7X_SUBLANES, tf), F32),
        ],
        compiler_params=_params(2),
        name="conv_ffn",
    )(h, g_pre.reshape(depth, 1, d), w_up, w_up, conv_w, conv_w, conv_b3, conv_b3, w_down,
      g_post.reshape(depth, 1, d))


def _qkv_kernel(h_ref, gq_ref, gkv_ref, wq_ref, wk_ref, wv_ref, q_ref, k_ref, vt_ref,
                xq_ref, xkv_ref):
    j = pl.program_id(1)

    @pl.when(j == 0)
    def _():
        x = h_ref[...]
        xs = x * lax.rsqrt(jnp.mean(x * x, axis=-1, keepdims=True) + EPS)
        xq_ref[...] = (xs * gq_ref[...]).astype(BF16)
        xkv_ref[...] = (xs * gkv_ref[...]).astype(BF16)

    scale = LOG2_E * HEAD_DIM ** -0.5
    q = _dot(xq_ref[...], wq_ref[...]) * scale
    k = _dot(xkv_ref[...], wk_ref[...])
    v = _dot(xkv_ref[...], wv_ref[...])
    for hh in range(q.shape[1] // HEAD_DIM):
        hs = slice(hh * HEAD_DIM, (hh + 1) * HEAD_DIM)
        q_ref[hh] = q[:, hs].astype(BF16)
        k_ref[hh] = k[:, hs].astype(BF16)
        vt_ref[hh] = v[:, hs].T.astype(BF16)


def _qkv(h, g_q, g_kv, w_q, w_k, w_v, seq_len):
    t, d = h.shape
    d_b = w_q.shape[1]
    n_heads = d_b // HEAD_DIM
    tm, tn = _tiles(seq_len).rows, _tiles(seq_len).cols
    hpt = tn // HEAD_DIM
    w_spec = pl.BlockSpec((d, tn), lambda m, j: (0, j))
    vec_spec = pl.BlockSpec((1, d), lambda m, j: (0, 0))
    hd_spec = pl.BlockSpec((hpt, tm, HEAD_DIM), lambda m, j: (j, m, 0))
    return pl.pallas_call(
        _qkv_kernel,
        grid=(t // tm, d_b // tn),
        in_specs=[pl.BlockSpec((tm, d), lambda m, j: (m, 0), pipeline_mode=pl.Buffered(1)),
                  vec_spec, vec_spec, w_spec, w_spec, w_spec],
        out_specs=[hd_spec, hd_spec,
                   pl.BlockSpec((hpt, HEAD_DIM, tm), lambda m, j: (j, 0, m))],
        out_shape=[jax.ShapeDtypeStruct((n_heads, t, HEAD_DIM), BF16),
                   jax.ShapeDtypeStruct((n_heads, t, HEAD_DIM), BF16),
                   jax.ShapeDtypeStruct((n_heads, HEAD_DIM, t), BF16)],
        scratch_shapes=[pltpu.VMEM((tm, d), BF16), pltpu.VMEM((tm, d), BF16)],
        compiler_params=_params(2),
        name="qkv_proj",
    )(h, g_q.reshape(1, d), g_kv.reshape(1, d), w_q, w_k, w_v)


def _attn_kernel(q_ref, k_ref, vt_ref, later_ref, o_ref, acc_ref, *, tq, hp):
    i = pl.program_id(2)
    key = lax.broadcasted_iota(jnp.int32, (tq, tq), 0)
    qry = lax.broadcasted_iota(jnp.int32, (tq, tq), 1)
    before_query = key < qry

    def scores(hh, jb, diagonal):
        start = pl.multiple_of(jb * tq, tq)
        ks = k_ref[hh, pl.ds(start, tq), :]
        z = lax.dot_general(ks, q_ref[hh], (((1,), (1,)), ((), ())),
                            preferred_element_type=F32)
        m = jnp.maximum(z, 0.0)
        zm = z - m
        l = jnp.log(1.0 + jnp.exp2(zm - m)) * LOG2_E
        sp = m + l
        if diagonal:
            sp = jnp.where(before_query, sp, 0.0)
        sp_hi = sp.astype(BF16)
        sp_lo = (sp - sp_hi.astype(F32)).astype(BF16)
        return jnp.concatenate([sp_hi, sp_lo], axis=0), z

    def weights(split, z, carry, diagonal):
        suffix = _dot(later_ref[...], split)
        a = jnp.exp2(z + carry - suffix)
        if diagonal:
            a = jnp.where(before_query, a, 0.0)
        return a.astype(BF16), carry - suffix[0:1]

    def values(hh, jb, a, first=False, keep=None):
        start = pl.multiple_of(jb * tq, tq)
        vt = vt_ref[hh, :, pl.ds(start, tq)]
        if keep is not None:
            vt = jnp.where(keep, vt, jnp.zeros_like(vt))
        pv = _dot(vt, a)
        if first:
            acc_ref[hh] = pv
        else:
            acc_ref[hh] += pv

    heads = range(hp)
    prev = jnp.maximum(i - 1, 0)
    sd = [scores(hh, i, True) for hh in heads]
    sv = [scores(hh, prev, False) for hh in heads]
    wd = [weights(*sd[hh], jnp.zeros((1, tq), F32), True) for hh in heads]
    wv = [weights(*sv[hh], wd[hh][1], False) for hh in heads]
    for hh in heads:
        values(hh, i, wd[hh][0], first=True)
    for hh in heads:
        values(hh, prev, wv[hh][0], keep=i > 0)
    carries = tuple(wv[hh][1] for hh in heads)

    least_used = functools.reduce(jnp.maximum, [jnp.max(c) for c in carries])
    remaining = jnp.where(least_used > -ATTN_LOG2_UNDERFLOW, i - 1, 0)

    def body(n, cs):
        jb = i - 2 - n
        s = [scores(hh, jb, False) for hh in heads]
        w = [weights(*s[hh], cs[hh], False) for hh in heads]
        for hh in heads:
            values(hh, jb, w[hh][0])
        return tuple(w[hh][1] for hh in heads)

    lax.fori_loop(0, remaining, body, carries)
    o_ref[...] = jnp.concatenate([acc_ref[hh].T for hh in heads], axis=1).astype(o_ref.dtype)


def _attention(q, k, vt, batch, seq_len):
    n_heads, t, dh = q.shape
    tq = _tiles(seq_len).attn
    hp = min(ATTN_HEADS_PER_STEP, n_heads)
    assert n_heads % hp == 0
    nq = seq_len // tq
    tri = jnp.triu(jnp.ones((tq, tq), BF16))
    later = jnp.concatenate([tri, tri], axis=1)
    return pl.pallas_call(
        functools.partial(_attn_kernel, tq=tq, hp=hp),
        grid=(batch, n_heads // hp, nq),
        in_specs=[
            pl.BlockSpec((hp, tq, dh), lambda b, g, i: (g, b * nq + i, 0)),
            pl.BlockSpec((hp, seq_len, dh), lambda b, g, i: (g, b, 0)),
            pl.BlockSpec((hp, dh, seq_len), lambda b, g, i: (g, 0, b)),
            pl.BlockSpec((tq, 2 * tq), lambda b, g, i: (0, 0)),
        ],
        out_specs=pl.BlockSpec((tq, hp * dh), lambda b, g, i: (b * nq + i, g)),
        out_shape=jax.ShapeDtypeStruct((t, n_heads * dh), BF16),
        scratch_shapes=[pltpu.VMEM((hp, dh, tq), F32)],
        compiler_params=_params(3),
        name="stickbreak_attn",
    )(q, k, vt, later)


def _proj_kernel(a_ref, w_ref, h_ref, g_ref, o_ref):
    o_ref[...] = h_ref[...] + _rms(_dot(a_ref[...], w_ref[...]), g_ref[...])


def _proj_residual(a, w, h, g, seq_len):
    t, d = h.shape
    d_in = a.shape[1]
    tm = _tiles(seq_len).rows
    return pl.pallas_call(
        _proj_kernel,
        grid=(t // tm,),
        in_specs=[pl.BlockSpec((tm, d_in), lambda m: (m, 0)),
                  pl.BlockSpec((d_in, d), lambda m: (0, 0), pipeline_mode=pl.Buffered(1)),
                  pl.BlockSpec((tm, d), lambda m: (m, 0)),
                  pl.BlockSpec((1, d), lambda m: (0, 0))],
        out_specs=pl.BlockSpec((tm, d), lambda m: (m, 0)),
        out_shape=jax.ShapeDtypeStruct((t, d), F32),
        compiler_params=_params(1),
        name="attn_out_proj",
    )(a, w, h, g.reshape(1, d))


def kernel(x, pre_mix_g, post_mix_g, pre_ffn_g, post_ffn_g, a_w_in, a_v_norm_g, a_w_spatial, a_b_spatial, a_w_out, kv_norm_g, w_k, w_v, b_w_q, b_w_o, ffn_w_up, ffn_conv_w, ffn_conv_b, ffn_w_down):
    batch, seq_len, d = x.shape
    n_a = a_w_in.shape[0]
    n_b = b_w_q.shape[0]
    assert n_a == 1 and n_b == 1 and pre_mix_g.shape[0] == n_a + n_b
    assert a_w_spatial.shape[2] == CHUNK and w_k.shape[1] % HEAD_DIM == 0
    h = x.reshape(batch * seq_len, d)

    h = _gmlp_layer(h, pre_mix_g[0], a_w_in[0].astype(BF16), a_v_norm_g[0], a_w_spatial[0],
                    a_b_spatial[0], a_w_out[0].astype(BF16), post_mix_g[0], seq_len)
    w_up, w_down = ffn_w_up.astype(BF16), ffn_w_down.astype(BF16)
    ffn = functools.partial(_ffn_layer, g_pre=pre_ffn_g, w_up=w_up, conv_w=ffn_conv_w,
                            conv_b=ffn_conv_b, w_down=w_down, g_post=post_ffn_g, seq_len=seq_len)
    h = ffn(h, 0)

    q, k, vt = _qkv(h, pre_mix_g[1], kv_norm_g, b_w_q[0].astype(BF16), w_k.astype(BF16),
                    w_v.astype(BF16), seq_len)
    att = _attention(q, k, vt, batch, seq_len)
    h = _proj_residual(att, b_w_o[0].astype(BF16), h, post_mix_g[1], seq_len)
    h = ffn(h, 1)
    return h.reshape(batch, seq_len, d)
```

```python
import functools
from typing import NamedTuple

import jax
import jax.numpy as jnp
from jax import lax
from jax.experimental import pallas as pl
from jax.experimental.pallas import tpu as pltpu

CHUNK = 128
HEAD_DIM = 128
EPS = 1e-6
LOG2_E = 1.4426950408889634
ATTN_HEADS_PER_STEP = 8
ATTN_LOG2_UNDERFLOW = 160.0

V7X_LANES = 128
V7X_SUBLANES = 8
V7X_VMEM_LIMIT_BYTES = 60000 * 1024
V7X_VMEM_BYTES = 64 * 1024 * 1024

F32 = jnp.float32
BF16 = jnp.bfloat16


class _Tiles(NamedTuple):
    rows: int
    ffn_rows: int
    cols: int
    attn: int


def _tiles(seq_len):
    tiles = _Tiles(rows=512, ffn_rows=1024, cols=512, attn=256)
    assert all(seq_len % n == 0 for n in (tiles.rows, tiles.ffn_rows, tiles.attn))
    assert tiles.rows % CHUNK == 0
    return tiles


def _params(n_axes, vmem=V7X_VMEM_LIMIT_BYTES):
    return pltpu.CompilerParams(
        dimension_semantics=("arbitrary",) * n_axes,
        vmem_limit_bytes=vmem,
    )


def _rms(x, g):
    ms = jnp.mean(x * x, axis=-1, keepdims=True)
    return x * lax.rsqrt(ms + EPS) * g


def _dot(a, b):
    return jnp.dot(a, b, preferred_element_type=F32)


def _gmlp_kernel(h_ref, gpre_ref, wu_ref, wv_ref, vg_ref, ws_ref, bs_ref, wo_ref,
                 gpost_ref, o_ref, xn_ref):
    j = pl.program_id(1)

    @pl.when(j == 0)
    def _():
        xn_ref[...] = _rms(h_ref[...], gpre_ref[...]).astype(BF16)
        o_ref[...] = jnp.zeros_like(o_ref)

    xn = xn_ref[...]
    u = jax.nn.gelu(_dot(xn, wu_ref[...]))
    v = jax.nn.gelu(_dot(xn, wv_ref[...]))
    tm, tn = u.shape
    row = lax.broadcasted_iota(jnp.int32, (CHUNK, CHUNK), 0)
    col = lax.broadcasted_iota(jnp.int32, (CHUNK, CHUNK), 1)
    causal = row >= col
    cols = []
    for g in range(tn // CHUNK):
        gs = slice(g * CHUNK, (g + 1) * CHUNK)
        vn = _rms(v[:, gs], vg_ref[:, gs]).astype(BF16)
        w = jnp.where(causal, ws_ref[g], 0.0).astype(BF16)
        b = bs_ref[g]
        rows = []
        for c in range(tm // CHUNK):
            cs = slice(c * CHUNK, (c + 1) * CHUNK)
            mixed = _dot(w, vn[cs, :]) + b
            rows.append(u[cs, gs] * mixed)
        cols.append(jnp.concatenate(rows, axis=0))
    gated = jnp.concatenate(cols, axis=1).astype(BF16)
    o_ref[...] += _dot(gated, wo_ref[...])

    @pl.when(j == pl.num_programs(1) - 1)
    def _():
        o_ref[...] = h_ref[...] + _rms(o_ref[...], gpost_ref[...])


def _gmlp_layer(h, g_pre, w_in, v_g, w_s, b_s, w_out, g_post, seq_len):
    t, d = h.shape
    d_a = w_out.shape[0]
    tm, tn = _tiles(seq_len).rows, _tiles(seq_len).cols
    nj = d_a // tn
    gpt = tn // CHUNK
    row = lambda m, j: (m, 0)
    vec_spec = pl.BlockSpec((1, d), lambda m, j: (0, 0))
    return pl.pallas_call(
        _gmlp_kernel,
        grid=(t // tm, nj),
        in_specs=[
            pl.BlockSpec((tm, d), row),
            vec_spec,
            pl.BlockSpec((d, tn), lambda m, j: (0, j)),
            pl.BlockSpec((d, tn), lambda m, j: (0, j + nj)),
            pl.BlockSpec((1, tn), lambda m, j: (0, j)),
            pl.BlockSpec((gpt, CHUNK, CHUNK), lambda m, j: (j, 0, 0)),
            pl.BlockSpec((gpt, CHUNK, 1), lambda m, j: (j, 0, 0)),
            pl.BlockSpec((tn, d), lambda m, j: (j, 0)),
            vec_spec,
        ],
        out_specs=pl.BlockSpec((tm, d), row),
        out_shape=jax.ShapeDtypeStruct((t, d), F32),
        scratch_shapes=[pltpu.VMEM((tm, d), BF16)],
        compiler_params=_params(2),
        name="gmlp_mixer",
    )(h, g_pre.reshape(1, d), w_in, w_in, v_g.reshape(1, d_a), w_s,
      b_s.reshape(b_s.shape[0], CHUNK, 1), w_out, g_post.reshape(1, d))


def _shift_rows(a, prev, s):
    r = pltpu.roll(a, s, axis=0)
    rp = pltpu.roll(prev, s, axis=0)
    sub = lax.broadcasted_iota(jnp.int32, prev.shape, 0)
    head = jnp.where(sub < s, rp, r[:V7X_SUBLANES])
    return jnp.concatenate([head, r[V7X_SUBLANES:]], axis=0)


def _causal_conv3(a, prev, cw, cb):
    return (cb + cw[2:3] * a + cw[1:2] * _shift_rows(a, prev, 1)
            + cw[0:1] * _shift_rows(a, prev, 2))


def _ffn_kernel(h_ref, gpre_ref, wg_ref, wv_ref, cwg_ref, cwv_ref, cbg_ref, cbv_ref,
                wd_ref, gpost_ref, o_ref, xn_ref, carry_ref, *, tiles_per_seq):
    m = pl.program_id(0)
    f = pl.program_id(1)

    @pl.when(f == 0)
    def _():
        xn_ref[...] = _rms(h_ref[...], gpre_ref[...]).astype(BF16)
        o_ref[...] = jnp.zeros_like(o_ref)

    @pl.when(m % tiles_per_seq == 0)
    def _():
        carry_ref[f] = jnp.zeros(carry_ref.shape[1:], F32)

    xn = xn_ref[...]
    ag = _dot(xn, wg_ref[...])
    av = _dot(xn, wv_ref[...])
    tm = ag.shape[0]
    prev_g = carry_ref[f, 0]
    prev_v = carry_ref[f, 1]
    carry_ref[f, 0] = ag[tm - V7X_SUBLANES:]
    carry_ref[f, 1] = av[tm - V7X_SUBLANES:]
    cg = _causal_conv3(ag, prev_g, cwg_ref[...], cbg_ref[...])
    cv = _causal_conv3(av, prev_v, cwv_ref[...], cbv_ref[...])
    act = (cg * jax.nn.sigmoid(cg) * cv).astype(BF16)
    o_ref[...] += _dot(act, wd_ref[...])

    @pl.when(f == pl.num_programs(1) - 1)
    def _():
        o_ref[...] = h_ref[...] + _rms(o_ref[...], gpost_ref[...])


def _ffn_layer(h, layer, g_pre, w_up, conv_w, conv_b, w_down, g_post, seq_len):
    t, d = h.shape
    depth, d_ff, _ = w_down.shape
    tm, tf = _tiles(seq_len).ffn_rows, _tiles(seq_len).cols
    nf = d_ff // tf
    assert d_ff % tf == 0 and conv_w.shape[1] == 3
    taps = conv_w.shape[1]
    row = lambda m, f: (m, 0)
    vec_spec = pl.BlockSpec((None, 1, d), lambda m, f: (layer, 0, 0))
    conv_b3 = conv_b.reshape(depth, 1, 2 * d_ff)
    return pl.pallas_call(
        functools.partial(_ffn_kernel, tiles_per_seq=seq_len // tm),
        grid=(t // tm, nf),
        in_specs=[
            pl.BlockSpec((tm, d), row),
            vec_spec,
            pl.BlockSpec((None, d, tf), lambda m, f: (layer, 0, f)),
            pl.BlockSpec((None, d, tf), lambda m, f: (layer, 0, f + nf)),
            pl.BlockSpec((None, taps, tf), lambda m, f: (layer, 0, f)),
            pl.BlockSpec((None, taps, tf), lambda m, f: (layer, 0, f + nf)),
            pl.BlockSpec((None, 1, tf), lambda m, f: (layer, 0, f)),
            pl.BlockSpec((None, 1, tf), lambda m, f: (layer, 0, f + nf)),
            pl.BlockSpec((None, tf, d), lambda m, f: (layer, f, 0)),
            vec_spec,
        ],
        out_specs=pl.BlockSpec((tm, d), row),
        out_shape=jax.ShapeDtypeStruct((t, d), F32),
        scratch_shapes=[
            pltpu.VMEM((tm, d), BF16),
            pltpu.VMEM((nf, 2, V7X_SUBLANES, tf), F32),
        ],
        compiler_params=_params(2, V7X_VMEM_BYTES),
        name="conv_ffn",
    )(h, g_pre.reshape(depth, 1, d), w_up, w_up, conv_w, conv_w, conv_b3, conv_b3, w_down,
      g_post.reshape(depth, 1, d))


def _qkv_kernel(h_ref, gq_ref, gkv_ref, wq_ref, wk_ref, wv_ref, q_ref, k_ref, vt_ref,
                xq_ref, xkv_ref):
    j = pl.program_id(1)

    @pl.when(j == 0)
    def _():
        x = h_ref[...]
        xs = x * lax.rsqrt(jnp.mean(x * x, axis=-1, keepdims=True) + EPS)
        xq_ref[...] = (xs * gq_ref[...]).astype(BF16)
        xkv_ref[...] = (xs * gkv_ref[...]).astype(BF16)

    scale = LOG2_E * HEAD_DIM ** -0.5
    q = _dot(xq_ref[...], wq_ref[...]) * scale
    k = _dot(xkv_ref[...], wk_ref[...])
    v = _dot(xkv_ref[...], wv_ref[...])
    for hh in range(q.shape[1] // HEAD_DIM):
        hs = slice(hh * HEAD_DIM, (hh + 1) * HEAD_DIM)
        q_ref[hh] = q[:, hs].astype(BF16)
        k_ref[hh] = k[:, hs].astype(BF16)
        vt_ref[hh] = v[:, hs].T.astype(BF16)


def _qkv(h, g_q, g_kv, w_q, w_k, w_v, seq_len):
    t, d = h.shape
    d_b = w_q.shape[1]
    n_heads = d_b // HEAD_DIM
    tm, tn = _tiles(seq_len).rows, _tiles(seq_len).cols
    hpt = tn // HEAD_DIM
    w_spec = pl.BlockSpec((d, tn), lambda m, j: (0, j))
    vec_spec = pl.BlockSpec((1, d), lambda m, j: (0, 0))
    hd_spec = pl.BlockSpec((hpt, tm, HEAD_DIM), lambda m, j: (j, m, 0))
    return pl.pallas_call(
        _qkv_kernel,
        grid=(t // tm, d_b // tn),
        in_specs=[pl.BlockSpec((tm, d), lambda m, j: (m, 0)), vec_spec, vec_spec,
                  w_spec, w_spec, w_spec],
        out_specs=[hd_spec, hd_spec,
                   pl.BlockSpec((hpt, HEAD_DIM, tm), lambda m, j: (j, 0, m))],
        out_shape=[jax.ShapeDtypeStruct((n_heads, t, HEAD_DIM), BF16),
                   jax.ShapeDtypeStruct((n_heads, t, HEAD_DIM), BF16),
                   jax.ShapeDtypeStruct((n_heads, HEAD_DIM, t), BF16)],
        scratch_shapes=[pltpu.VMEM((tm, d), BF16), pltpu.VMEM((tm, d), BF16)],
        compiler_params=_params(2),
        name="qkv_proj",
    )(h, g_q.reshape(1, d), g_kv.reshape(1, d), w_q, w_k, w_v)


def _attn_kernel(q_ref, k_ref, vt_ref, later_ref, o_ref, acc_ref, *, tq, hp):
    i = pl.program_id(2)
    key = lax.broadcasted_iota(jnp.int32, (tq, tq), 0)
    qry = lax.broadcasted_iota(jnp.int32, (tq, tq), 1)
    before_query = key < qry

    def scores(hh, jb, diagonal):
        start = pl.multiple_of(jb * tq, tq)
        ks = k_ref[hh, pl.ds(start, tq), :]
        z = lax.dot_general(ks, q_ref[hh], (((1,), (1,)), ((), ())),
                            preferred_element_type=F32)
        m = jnp.maximum(z, 0.0)
        zm = z - m
        l = jnp.log(1.0 + jnp.exp2(zm - m)) * LOG2_E
        sp = m + l
        if diagonal:
            sp = jnp.where(before_query, sp, 0.0)
        sp_hi = sp.astype(BF16)
        sp_lo = (sp - sp_hi.astype(F32)).astype(BF16)
        return jnp.concatenate([sp_hi, sp_lo], axis=0), z

    def weights(split, z, carry, diagonal):
        suffix = _dot(later_ref[...], split)
        a = jnp.exp2(z + carry - suffix)
        if diagonal:
            a = jnp.where(before_query, a, 0.0)
        return a.astype(BF16), carry - suffix[0:1]

    def values(hh, jb, a, first=False, keep=None):
        start = pl.multiple_of(jb * tq, tq)
        vt = vt_ref[hh, :, pl.ds(start, tq)]
        if keep is not None:
            vt = jnp.where(keep, vt, jnp.zeros_like(vt))
        pv = _dot(vt, a)
        if first:
            acc_ref[hh] = pv
        else:
            acc_ref[hh] += pv

    heads = range(hp)
    prev = jnp.maximum(i - 1, 0)
    sd = [scores(hh, i, True) for hh in heads]
    sv = [scores(hh, prev, False) for hh in heads]
    wd = [weights(*sd[hh], jnp.zeros((1, tq), F32), True) for hh in heads]
    wv = [weights(*sv[hh], wd[hh][1], False) for hh in heads]
    for hh in heads:
        values(hh, i, wd[hh][0], first=True)
    for hh in heads:
        values(hh, prev, wv[hh][0], keep=i > 0)
    carries = tuple(wv[hh][1] for hh in heads)

    least_used = functools.reduce(jnp.maximum, [jnp.max(c) for c in carries])
    remaining = jnp.where(least_used > -ATTN_LOG2_UNDERFLOW, i - 1, 0)

    def body(n, cs):
        jb = i - 2 - n
        s = [scores(hh, jb, False) for hh in heads]
        w = [weights(*s[hh], cs[hh], False) for hh in heads]
        for hh in heads:
            values(hh, jb, w[hh][0])
        return tuple(w[hh][1] for hh in heads)

    lax.fori_loop(0, remaining, body, carries)
    o_ref[...] = jnp.concatenate([acc_ref[hh].T for hh in heads], axis=1).astype(o_ref.dtype)


def _attention(q, k, vt, batch, seq_len):
    n_heads, t, dh = q.shape
    tq = _tiles(seq_len).attn
    hp = min(ATTN_HEADS_PER_STEP, n_heads)
    assert n_heads % hp == 0
    nq = seq_len // tq
    tri = jnp.triu(jnp.ones((tq, tq), BF16))
    later = jnp.concatenate([tri, tri], axis=1)
    return pl.pallas_call(
        functools.partial(_attn_kernel, tq=tq, hp=hp),
        grid=(batch, n_heads // hp, nq),
        in_specs=[
            pl.BlockSpec((hp, tq, dh), lambda b, g, i: (g, b * nq + i, 0)),
            pl.BlockSpec((hp, seq_len, dh), lambda b, g, i: (g, b, 0)),
            pl.BlockSpec((hp, dh, seq_len), lambda b, g, i: (g, 0, b)),
            pl.BlockSpec((tq, 2 * tq), lambda b, g, i: (0, 0)),
        ],
        out_specs=pl.BlockSpec((tq, hp * dh), lambda b, g, i: (b * nq + i, g)),
        out_shape=jax.ShapeDtypeStruct((t, n_heads * dh), BF16),
        scratch_shapes=[pltpu.VMEM((hp, dh, tq), F32)],
        compiler_params=_params(3),
        name="stickbreak_attn",
    )(q, k, vt, later)


def _proj_kernel(a_ref, w_ref, h_ref, g_ref, o_ref):
    o_ref[...] = h_ref[...] + _rms(_dot(a_ref[...], w_ref[...]), g_ref[...])


def _proj_residual(a, w, h, g, seq_len):
    t, d = h.shape
    d_in = a.shape[1]
    tm = _tiles(seq_len).rows
    return pl.pallas_call(
        _proj_kernel,
        grid=(t // tm,),
        in_specs=[pl.BlockSpec((tm, d_in), lambda m: (m, 0)),
                  pl.BlockSpec((d_in, d), lambda m: (0, 0)),
                  pl.BlockSpec((tm, d), lambda m: (m, 0)),
                  pl.BlockSpec((1, d), lambda m: (0, 0))],
        out_specs=pl.BlockSpec((tm, d), lambda m: (m, 0)),
        out_shape=jax.ShapeDtypeStruct((t, d), F32),
        compiler_params=_params(1),
        name="attn_out_proj",
    )(a, w, h, g.reshape(1, d))


def kernel(x, pre_mix_g, post_mix_g, pre_ffn_g, post_ffn_g, a_w_in, a_v_norm_g, a_w_spatial, a_b_spatial, a_w_out, kv_norm_g, w_k, w_v, b_w_q, b_w_o, ffn_w_up, ffn_conv_w, ffn_conv_b, ffn_w_down):
    batch, seq_len, d = x.shape
    n_a = a_w_in.shape[0]
    n_b = b_w_q.shape[0]
    assert n_a == 1 and n_b == 1 and pre_mix_g.shape[0] == n_a + n_b
    assert a_w_spatial.shape[2] == CHUNK and w_k.shape[1] % HEAD_DIM == 0
    h = x.reshape(batch * seq_len, d)

    h = _gmlp_layer(h, pre_mix_g[0], a_w_in[0].astype(BF16), a_v_norm_g[0], a_w_spatial[0],
                    a_b_spatial[0], a_w_out[0].astype(BF16), post_mix_g[0], seq_len)
    w_up, w_down = ffn_w_up.astype(BF16), ffn_w_down.astype(BF16)
    ffn = functools.partial(_ffn_layer, g_pre=pre_ffn_g, w_up=w_up, conv_w=ffn_conv_w,
                            conv_b=ffn_conv_b, w_down=w_down, g_post=post_ffn_g, seq_len=seq_len)
    h = ffn(h, 0)

    q, k, vt = _qkv(h, pre_mix_g[1], kv_norm_g, b_w_q[0].astype(BF16), w_k.astype(BF16),
                    w_v.astype(BF16), seq_len)
    att = _attention(q, k, vt, batch, seq_len)
    h = _proj_residual(att, b_w_o[0].astype(BF16), h, post_mix_g[1], seq_len)
    h = ffn(h, 1)
    return h.reshape(batch, seq_len, d)
```

```python
import functools
from typing import NamedTuple

import jax
import jax.numpy as jnp
from jax import lax
from jax.experimental import pallas as pl
from jax.experimental.pallas import tpu as pltpu

CHUNK = 128
HEAD_DIM = 128
EPS = 1e-6
LOG2_E = 1.4426950408889634
ATTN_HEADS_PER_STEP = 8
ATTN_LOG2_UNDERFLOW = 160.0

V7X_LANES = 128
V7X_SUBLANES = 8
V7X_VMEM_LIMIT_BYTES = 60000 * 1024

NORM_ROW_BLOCK = V7X_SUBLANES
PRENORM_ROW_BLOCK = 2 * V7X_SUBLANES

F32 = jnp.float32
BF16 = jnp.bfloat16


class _Tiles(NamedTuple):
    rows: int
    proj_rows: int
    cols: int
    attn: int


def _tiles(seq_len):
    tiles = _Tiles(rows=1024, proj_rows=512, cols=512, attn=256)
    assert all(seq_len % n == 0 for n in (tiles.rows, tiles.proj_rows, tiles.attn))
    assert tiles.rows % CHUNK == 0
    return tiles


def _params(n_axes):
    return pltpu.CompilerParams(
        dimension_semantics=("arbitrary",) * n_axes,
        vmem_limit_bytes=V7X_VMEM_LIMIT_BYTES,
    )


def _rms(x, g):
    ms = jnp.mean(x * x, axis=-1, keepdims=True)
    return x * lax.rsqrt(ms + EPS) * g


def _residual_rms_rows(o_ref, h_ref, g_ref):
    g = g_ref[...]
    for r0 in range(0, o_ref.shape[0], NORM_ROW_BLOCK):
        rows = slice(r0, r0 + NORM_ROW_BLOCK)
        o_ref[rows, :] = h_ref[rows, :] + _rms(o_ref[rows, :], g)


def _rms_rows(dst_ref, x_ref, g_ref):
    g = g_ref[...]
    for r0 in range(0, x_ref.shape[0], PRENORM_ROW_BLOCK):
        rows = slice(r0, r0 + PRENORM_ROW_BLOCK)
        dst_ref[rows, :] = _rms(x_ref[rows, :], g).astype(dst_ref.dtype)


def _dot(a, b):
    return jnp.dot(a, b, preferred_element_type=F32)


def _gmlp_kernel(h_ref, gpre_ref, wu_ref, wv_ref, vg_ref, ws_ref, bs_ref, wo_ref,
                 gpost_ref, o_ref, xn_ref):
    j = pl.program_id(1)

    @pl.when(j == 0)
    def _():
        _rms_rows(xn_ref, h_ref, gpre_ref)
        o_ref[...] = jnp.zeros_like(o_ref)

    xn = xn_ref[...]
    u = jax.nn.gelu(_dot(xn, wu_ref[...]))
    v = jax.nn.gelu(_dot(xn, wv_ref[...]))
    tm, tn = u.shape
    row = lax.broadcasted_iota(jnp.int32, (CHUNK, CHUNK), 0)
    col = lax.broadcasted_iota(jnp.int32, (CHUNK, CHUNK), 1)
    causal = row >= col
    cols = []
    for g in range(tn // CHUNK):
        gs = slice(g * CHUNK, (g + 1) * CHUNK)
        vn = _rms(v[:, gs], vg_ref[:, gs]).astype(BF16)
        w = jnp.where(causal, ws_ref[g], 0.0).astype(BF16)
        b = bs_ref[g]
        rows = []
        for c in range(tm // CHUNK):
            cs = slice(c * CHUNK, (c + 1) * CHUNK)
            mixed = _dot(w, vn[cs, :]) + b
            rows.append(u[cs, gs] * mixed)
        cols.append(jnp.concatenate(rows, axis=0))
    gated = jnp.concatenate(cols, axis=1).astype(BF16)
    o_ref[...] += _dot(gated, wo_ref[...])

    @pl.when(j == pl.num_programs(1) - 1)
    def _():
        _residual_rms_rows(o_ref, h_ref, gpost_ref)


def _gmlp_layer(h, g_pre, w_in, v_g, w_s, b_s, w_out, g_post, seq_len):
    t, d = h.shape
    d_a = w_out.shape[0]
    tm, tn = _tiles(seq_len).rows, _tiles(seq_len).cols
    nj = d_a // tn
    gpt = tn // CHUNK
    row = lambda m, j: (m, 0)
    vec_spec = pl.BlockSpec((1, d), lambda m, j: (0, 0))
    return pl.pallas_call(
        _gmlp_kernel,
        grid=(t // tm, nj),
        in_specs=[
            pl.BlockSpec((tm, d), row),
            vec_spec,
            pl.BlockSpec((d, tn), lambda m, j: (0, j)),
            pl.BlockSpec((d, tn), lambda m, j: (0, j + nj)),
            pl.BlockSpec((1, tn), lambda m, j: (0, j)),
            pl.BlockSpec((gpt, CHUNK, CHUNK), lambda m, j: (j, 0, 0)),
            pl.BlockSpec((gpt, CHUNK, 1), lambda m, j: (j, 0, 0)),
            pl.BlockSpec((tn, d), lambda m, j: (j, 0)),
            vec_spec,
        ],
        out_specs=pl.BlockSpec((tm, d), row),
        out_shape=jax.ShapeDtypeStruct((t, d), F32),
        scratch_shapes=[pltpu.VMEM((tm, d), BF16)],
        compiler_params=_params(2),
        name="gmlp_mixer",
    )(h, g_pre.reshape(1, d), w_in, w_in, v_g.reshape(1, d_a), w_s,
      b_s.reshape(b_s.shape[0], CHUNK, 1), w_out, g_post.reshape(1, d))


def _ffn_kernel(h_ref, gpre_ref, wg_ref, wv_ref, cwg_ref, cwv_ref, cbg_ref, cbv_ref,
                wd_ref, gpost_ref, o_ref, xn_ref, carry_ref, ext_ref, *, tiles_per_seq):
    m = pl.program_id(0)
    f = pl.program_id(1)

    @pl.when(f == 0)
    def _():
        _rms_rows(xn_ref, h_ref, gpre_ref)
        o_ref[...] = jnp.zeros_like(o_ref)

    @pl.when(m % tiles_per_seq == 0)
    def _():
        carry_ref[f] = jnp.zeros(carry_ref.shape[1:], F32)

    xn = xn_ref[...]
    tm = xn.shape[0]
    pad = V7X_SUBLANES
    ext_ref[:, 0:pad, :] = carry_ref[f]
    ext_ref[0, pad:, :] = _dot(xn, wg_ref[...])
    ext_ref[1, pad:, :] = _dot(xn, wv_ref[...])
    carry_ref[f] = ext_ref[:, tm:, :]

    def conv(which, cw, cb):
        taps = [ext_ref[which, pad - s:pad - s + tm, :] for s in (2, 1, 0)]
        return cb + cw[0:1] * taps[0] + cw[1:2] * taps[1] + cw[2:3] * taps[2]

    cg = conv(0, cwg_ref[...], cbg_ref[...])
    cv = conv(1, cwv_ref[...], cbv_ref[...])
    act = (cg * jax.nn.sigmoid(cg) * cv).astype(BF16)
    o_ref[...] += _dot(act, wd_ref[...])

    @pl.when(f == pl.num_programs(1) - 1)
    def _():
        _residual_rms_rows(o_ref, h_ref, gpost_ref)


def _ffn_layer(h, layer, g_pre, w_up, conv_w, conv_b, w_down, g_post, seq_len):
    t, d = h.shape
    depth, d_ff, _ = w_down.shape
    tm, tf = _tiles(seq_len).rows, _tiles(seq_len).cols
    nf = d_ff // tf
    assert d_ff % tf == 0 and conv_w.shape[1] == 3
    taps = conv_w.shape[1]
    row = lambda m, f: (m, 0)
    vec_spec = pl.BlockSpec((None, 1, d), lambda m, f: (layer, 0, 0))
    conv_b3 = conv_b.reshape(depth, 1, 2 * d_ff)
    return pl.pallas_call(
        functools.partial(_ffn_kernel, tiles_per_seq=seq_len // tm),
        grid=(t // tm, nf),
        in_specs=[
            pl.BlockSpec((tm, d), row),
            vec_spec,
            pl.BlockSpec((None, d, tf), lambda m, f: (layer, 0, f)),
            pl.BlockSpec((None, d, tf), lambda m, f: (layer, 0, f + nf)),
            pl.BlockSpec((None, taps, tf), lambda m, f: (layer, 0, f)),
            pl.BlockSpec((None, taps, tf), lambda m, f: (layer, 0, f + nf)),
            pl.BlockSpec((None, 1, tf), lambda m, f: (layer, 0, f)),
            pl.BlockSpec((None, 1, tf), lambda m, f: (layer, 0, f + nf)),
            pl.BlockSpec((None, tf, d), lambda m, f: (layer, f, 0)),
            vec_spec,
        ],
        out_specs=pl.BlockSpec((tm, d), row),
        out_shape=jax.ShapeDtypeStruct((t, d), F32),
        scratch_shapes=[
            pltpu.VMEM((tm, d), BF16),
            pltpu.VMEM((nf, 2, V7X_SUBLANES, tf), F32),
            pltpu.VMEM((2, tm + V7X_SUBLANES, tf), F32),
        ],
        compiler_params=_params(2),
        name="conv_ffn",
    )(h, g_pre.reshape(depth, 1, d), w_up, w_up, conv_w, conv_w, conv_b3, conv_b3, w_down,
      g_post.reshape(depth, 1, d))


def _qkv_kernel(h_ref, gq_ref, gkv_ref, wq_ref, wk_ref, wv_ref, q_ref, k_ref, vt_ref,
                xq_ref, xkv_ref):
    j = pl.program_id(1)

    @pl.when(j == 0)
    def _():
        gq, gkv = gq_ref[...], gkv_ref[...]
        for r0 in range(0, h_ref.shape[0], PRENORM_ROW_BLOCK):
            rows = slice(r0, r0 + PRENORM_ROW_BLOCK)
            x = h_ref[rows, :]
            xs = x * lax.rsqrt(jnp.mean(x * x, axis=-1, keepdims=True) + EPS)
            xq_ref[rows, :] = (xs * gq).astype(BF16)
            xkv_ref[rows, :] = (xs * gkv).astype(BF16)

    scale = LOG2_E * HEAD_DIM ** -0.5
    q = _dot(xq_ref[...], wq_ref[...]) * scale
    k = _dot(xkv_ref[...], wk_ref[...])
    v = _dot(xkv_ref[...], wv_ref[...])
    for hh in range(q.shape[1] // HEAD_DIM):
        hs = slice(hh * HEAD_DIM, (hh + 1) * HEAD_DIM)
        q_ref[hh] = q[:, hs].astype(BF16)
        k_ref[hh] = k[:, hs].astype(BF16)
        vt_ref[hh] = v[:, hs].T.astype(BF16)


def _qkv(h, g_q, g_kv, w_q, w_k, w_v, seq_len):
    t, d = h.shape
    d_b = w_q.shape[1]
    n_heads = d_b // HEAD_DIM
    tm, tn = _tiles(seq_len).rows, _tiles(seq_len).cols
    hpt = tn // HEAD_DIM
    w_spec = pl.BlockSpec((d, tn), lambda m, j: (0, j))
    vec_spec = pl.BlockSpec((1, d), lambda m, j: (0, 0))
    hd_spec = pl.BlockSpec((hpt, tm, HEAD_DIM), lambda m, j: (j, m, 0))
    return pl.pallas_call(
        _qkv_kernel,
        grid=(t // tm, d_b // tn),
        in_specs=[pl.BlockSpec((tm, d), lambda m, j: (m, 0)), vec_spec, vec_spec,
                  w_spec, w_spec, w_spec],
        out_specs=[hd_spec, hd_spec,
                   pl.BlockSpec((hpt, HEAD_DIM, tm), lambda m, j: (j, 0, m))],
        out_shape=[jax.ShapeDtypeStruct((n_heads, t, HEAD_DIM), BF16),
                   jax.ShapeDtypeStruct((n_heads, t, HEAD_DIM), BF16),
                   jax.ShapeDtypeStruct((n_heads, HEAD_DIM, t), BF16)],
        scratch_shapes=[pltpu.VMEM((tm, d), BF16), pltpu.VMEM((tm, d), BF16)],
        compiler_params=_params(2),
        name="qkv_proj",
    )(h, g_q.reshape(1, d), g_kv.reshape(1, d), w_q, w_k, w_v)


def _attn_kernel(q_ref, k_ref, vt_ref, later_ref, o_ref, acc_ref, *, tq, hp):
    i = pl.program_id(2)
    key = lax.broadcasted_iota(jnp.int32, (tq, tq), 0)
    qry = lax.broadcasted_iota(jnp.int32, (tq, tq), 1)
    before_query = key < qry

    def scores(hh, jb, diagonal):
        start = pl.multiple_of(jb * tq, tq)
        ks = k_ref[hh, pl.ds(start, tq), :]
        z = lax.dot_general(ks, q_ref[hh], (((1,), (1,)), ((), ())),
                            preferred_element_type=F32)
        m = jnp.maximum(z, 0.0)
        zm = z - m
        l = jnp.log(1.0 + jnp.exp2(zm - m)) * LOG2_E
        sp = m + l
        if diagonal:
            sp = jnp.where(before_query, sp, 0.0)
        sp_hi = sp.astype(BF16)
        sp_lo = (sp - sp_hi.astype(F32)).astype(BF16)
        return jnp.concatenate([sp_hi, sp_lo], axis=0), z

    def weights(split, z, carry, diagonal):
        suffix = _dot(later_ref[...], split)
        a = jnp.exp2(z + carry - suffix)
        if diagonal:
            a = jnp.where(before_query, a, 0.0)
        return a.astype(BF16), carry - suffix[0:1]

    def values(hh, jb, a, first=False, keep=None):
        start = pl.multiple_of(jb * tq, tq)
        vt = vt_ref[hh, :, pl.ds(start, tq)]
        if keep is not None:
            vt = jnp.where(keep, vt, jnp.zeros_like(vt))
        pv = _dot(vt, a)
        if first:
            acc_ref[hh] = pv
        else:
            acc_ref[hh] += pv

    heads = range(hp)
    prev = jnp.maximum(i - 1, 0)
    sd = [scores(hh, i, True) for hh in heads]
    sv = [scores(hh, prev, False) for hh in heads]
    wd = [weights(*sd[hh], jnp.zeros((1, tq), F32), True) for hh in heads]
    wv = [weights(*sv[hh], wd[hh][1], False) for hh in heads]
    for hh in heads:
        values(hh, i, wd[hh][0], first=True)
    for hh in heads:
        values(hh, prev, wv[hh][0], keep=i > 0)
    carries = tuple(wv[hh][1] for hh in heads)

    least_used = functools.reduce(jnp.maximum, [jnp.max(c) for c in carries])
    remaining = jnp.where(least_used > -ATTN_LOG2_UNDERFLOW, i - 1, 0)

    def body(n, cs):
        jb = i - 2 - n
        s = [scores(hh, jb, False) for hh in heads]
        w = [weights(*s[hh], cs[hh], False) for hh in heads]
        for hh in heads:
            values(hh, jb, w[hh][0])
        return tuple(w[hh][1] for hh in heads)

    lax.fori_loop(0, remaining, body, carries)
    o_ref[...] = jnp.concatenate([acc_ref[hh].T for hh in heads], axis=1).astype(o_ref.dtype)


def _attention(q, k, vt, batch, seq_len):
    n_heads, t, dh = q.shape
    tq = _tiles(seq_len).attn
    hp = min(ATTN_HEADS_PER_STEP, n_heads)
    assert n_heads % hp == 0
    nq = seq_len // tq
    tri = jnp.triu(jnp.ones((tq, tq), BF16))
    later = jnp.concatenate([tri, tri], axis=1)
    return pl.pallas_call(
        functools.partial(_attn_kernel, tq=tq, hp=hp),
        grid=(batch, n_heads // hp, nq),
        in_specs=[
            pl.BlockSpec((hp, tq, dh), lambda b, g, i: (g, b * nq + i, 0)),
            pl.BlockSpec((hp, seq_len, dh), lambda b, g, i: (g, b, 0)),
            pl.BlockSpec((hp, dh, seq_len), lambda b, g, i: (g, 0, b)),
            pl.BlockSpec((tq, 2 * tq), lambda b, g, i: (0, 0)),
        ],
        out_specs=pl.BlockSpec((tq, hp * dh), lambda b, g, i: (b * nq + i, g)),
        out_shape=jax.ShapeDtypeStruct((t, n_heads * dh), BF16),
        scratch_shapes=[pltpu.VMEM((hp, dh, tq), F32)],
        compiler_params=_params(3),
        name="stickbreak_attn",
    )(q, k, vt, later)


def _proj_kernel(a_ref, w_ref, h_ref, g_ref, o_ref):
    o_ref[...] = h_ref[...] + _rms(_dot(a_ref[...], w_ref[...]), g_ref[...])


def _proj_residual(a, w, h, g, seq_len):
    t, d = h.shape
    d_in = a.shape[1]
    tm = _tiles(seq_len).proj_rows
    return pl.pallas_call(
        _proj_kernel,
        grid=(t // tm,),
        in_specs=[pl.BlockSpec((tm, d_in), lambda m: (m, 0)),
                  pl.BlockSpec((d_in, d), lambda m: (0, 0)),
                  pl.BlockSpec((tm, d), lambda m: (m, 0)),
                  pl.BlockSpec((1, d), lambda m: (0, 0))],
        out_specs=pl.BlockSpec((tm, d), lambda m: (m, 0)),
        out_shape=jax.ShapeDtypeStruct((t, d), F32),
        compiler_params=_params(1),
        name="attn_out_proj",
    )(a, w, h, g.reshape(1, d))


def kernel(x, pre_mix_g, post_mix_g, pre_ffn_g, post_ffn_g, a_w_in, a_v_norm_g, a_w_spatial, a_b_spatial, a_w_out, kv_norm_g, w_k, w_v, b_w_q, b_w_o, ffn_w_up, ffn_conv_w, ffn_conv_b, ffn_w_down):
    batch, seq_len, d = x.shape
    n_a = a_w_in.shape[0]
    n_b = b_w_q.shape[0]
    assert n_a == 1 and n_b == 1 and pre_mix_g.shape[0] == n_a + n_b
    assert a_w_spatial.shape[2] == CHUNK and w_k.shape[1] % HEAD_DIM == 0
    h = x.reshape(batch * seq_len, d)

    h = _gmlp_layer(h, pre_mix_g[0], a_w_in[0].astype(BF16), a_v_norm_g[0], a_w_spatial[0],
                    a_b_spatial[0], a_w_out[0].astype(BF16), post_mix_g[0], seq_len)
    w_up, w_down = ffn_w_up.astype(BF16), ffn_w_down.astype(BF16)
    ffn = functools.partial(_ffn_layer, g_pre=pre_ffn_g, w_up=w_up, conv_w=ffn_conv_w,
                            conv_b=ffn_conv_b, w_down=w_down, g_post=post_ffn_g, seq_len=seq_len)
    h = ffn(h, 0)

    q, k, vt = _qkv(h, pre_mix_g[1], kv_norm_g, b_w_q[0].astype(BF16), w_k.astype(BF16),
                    w_v.astype(BF16), seq_len)
    att = _attention(q, k, vt, batch, seq_len)
    h = _proj_residual(att, b_w_o[0].astype(BF16), h, post_mix_g[1], seq_len)
    h = ffn(h, 1)
    return h.reshape(batch, seq_len, d)
```

```python
import functools
from typing import NamedTuple

import jax
import jax.numpy as jnp
from jax import lax
from jax.experimental import pallas as pl
from jax.experimental.pallas import tpu as pltpu

CHUNK = 128
HEAD_DIM = 128
EPS = 1e-6
LOG2_E = 1.4426950408889634
ATTN_HEADS_PER_STEP = 8
DOWN_PROJ_SPLITS = 2
ATTN_LOG2_UNDERFLOW = 160.0

V7X_LANES = 128
V7X_SUBLANES = 8
V7X_VMEM_LIMIT_BYTES = 60000 * 1024

NORM_ROW_BLOCK = V7X_SUBLANES
PRENORM_ROW_BLOCK = 2 * V7X_SUBLANES

F32 = jnp.float32
BF16 = jnp.bfloat16


class _Tiles(NamedTuple):
    rows: int
    proj_rows: int
    cols: int
    attn: int


def _tiles(seq_len):
    tiles = _Tiles(rows=1024, proj_rows=512, cols=512, attn=256)
    assert all(seq_len % n == 0 for n in (tiles.rows, tiles.proj_rows, tiles.attn))
    assert tiles.rows % CHUNK == 0
    return tiles


def _params(n_axes):
    return pltpu.CompilerParams(
        dimension_semantics=("arbitrary",) * n_axes,
        vmem_limit_bytes=V7X_VMEM_LIMIT_BYTES,
    )


def _rms(x, g):
    ms = jnp.mean(x * x, axis=-1, keepdims=True)
    return x * lax.rsqrt(ms + EPS) * g


def _residual_rms_rows(o_ref, h_ref, g_ref):
    g = g_ref[...]
    for r0 in range(0, o_ref.shape[0], NORM_ROW_BLOCK):
        rows = slice(r0, r0 + NORM_ROW_BLOCK)
        o_ref[rows, :] = h_ref[rows, :] + _rms(o_ref[rows, :], g)


def _rms_rows(dst_ref, x_ref, g_ref):
    g = g_ref[...]
    for r0 in range(0, x_ref.shape[0], PRENORM_ROW_BLOCK):
        rows = slice(r0, r0 + PRENORM_ROW_BLOCK)
        dst_ref[rows, :] = _rms(x_ref[rows, :], g).astype(dst_ref.dtype)


def _dot(a, b):
    return jnp.dot(a, b, preferred_element_type=F32)


def _gmlp_kernel(h_ref, gpre_ref, wu_ref, wv_ref, vg_ref, ws_ref, bs_ref, wo_ref,
                 gpost_ref, o_ref, xn_ref):
    j = pl.program_id(1)

    @pl.when(j == 0)
    def _():
        _rms_rows(xn_ref, h_ref, gpre_ref)
        o_ref[...] = jnp.zeros_like(o_ref)

    xn = xn_ref[...]
    u = jax.nn.gelu(_dot(xn, wu_ref[...]))
    v = jax.nn.gelu(_dot(xn, wv_ref[...]))
    tm, tn = u.shape
    row = lax.broadcasted_iota(jnp.int32, (CHUNK, CHUNK), 0)
    col = lax.broadcasted_iota(jnp.int32, (CHUNK, CHUNK), 1)
    causal = row >= col
    per_part = tn // CHUNK // DOWN_PROJ_SPLITS
    for part in range(DOWN_PROJ_SPLITS):
        cols = []
        for g in range(part * per_part, (part + 1) * per_part):
            gs = slice(g * CHUNK, (g + 1) * CHUNK)
            vn = _rms(v[:, gs], vg_ref[:, gs]).astype(BF16)
            w = jnp.where(causal, ws_ref[g], 0.0).astype(BF16)
            b = bs_ref[g]
            rows = []
            for c in range(tm // CHUNK):
                cs = slice(c * CHUNK, (c + 1) * CHUNK)
                mixed = _dot(w, vn[cs, :]) + b
                rows.append(u[cs, gs] * mixed)
            cols.append(jnp.concatenate(rows, axis=0))
        gated = jnp.concatenate(cols, axis=1).astype(BF16)
        ks = slice(part * per_part * CHUNK, (part + 1) * per_part * CHUNK)
        o_ref[...] += _dot(gated, wo_ref[ks, :])

    @pl.when(j == pl.num_programs(1) - 1)
    def _():
        _residual_rms_rows(o_ref, h_ref, gpost_ref)


def _gmlp_layer(h, g_pre, w_in, v_g, w_s, b_s, w_out, g_post, seq_len):
    t, d = h.shape
    d_a = w_out.shape[0]
    tm, tn = _tiles(seq_len).rows, _tiles(seq_len).cols
    nj = d_a // tn
    gpt = tn // CHUNK
    row = lambda m, j: (m, 0)
    vec_spec = pl.BlockSpec((1, d), lambda m, j: (0, 0))
    return pl.pallas_call(
        _gmlp_kernel,
        grid=(t // tm, nj),
        in_specs=[
            pl.BlockSpec((tm, d), row),
            vec_spec,
            pl.BlockSpec((d, tn), lambda m, j: (0, j)),
            pl.BlockSpec((d, tn), lambda m, j: (0, j + nj)),
            pl.BlockSpec((1, tn), lambda m, j: (0, j)),
            pl.BlockSpec((gpt, CHUNK, CHUNK), lambda m, j: (j, 0, 0)),
            pl.BlockSpec((gpt, CHUNK, 1), lambda m, j: (j, 0, 0)),
            pl.BlockSpec((tn, d), lambda m, j: (j, 0)),
            vec_spec,
        ],
        out_specs=pl.BlockSpec((tm, d), row),
        out_shape=jax.ShapeDtypeStruct((t, d), F32),
        scratch_shapes=[pltpu.VMEM((tm, d), BF16)],
        compiler_params=_params(2),
        name="gmlp_mixer",
    )(h, g_pre.reshape(1, d), w_in, w_in, v_g.reshape(1, d_a), w_s,
      b_s.reshape(b_s.shape[0], CHUNK, 1), w_out, g_post.reshape(1, d))


def _ffn_kernel(h_ref, gpre_ref, wg_ref, wv_ref, cwg_ref, cwv_ref, cbg_ref, cbv_ref,
                wd_ref, gpost_ref, o_ref, xn_ref, carry_ref, ext_ref, *, tiles_per_seq):
    m = pl.program_id(0)
    f = pl.program_id(1)

    @pl.when(f == 0)
    def _():
        _rms_rows(xn_ref, h_ref, gpre_ref)
        o_ref[...] = jnp.zeros_like(o_ref)

    @pl.when(m % tiles_per_seq == 0)
    def _():
        carry_ref[f] = jnp.zeros(carry_ref.shape[1:], F32)

    xn = xn_ref[...]
    tm = xn.shape[0]
    pad = V7X_SUBLANES
    ext_ref[:, 0:pad, :] = carry_ref[f]
    ext_ref[0, pad:, :] = _dot(xn, wg_ref[...])
    ext_ref[1, pad:, :] = _dot(xn, wv_ref[...])
    carry_ref[f] = ext_ref[:, tm:, :]

    def conv(which, cs, cw_ref, cb_ref):
        cw = cw_ref[:, cs]
        taps = [ext_ref[which, pad - s:pad - s + tm, cs] for s in (2, 1, 0)]
        return cb_ref[:, cs] + cw[0:1] * taps[0] + cw[1:2] * taps[1] + cw[2:3] * taps[2]

    width = wd_ref.shape[0] // DOWN_PROJ_SPLITS
    for part in range(DOWN_PROJ_SPLITS):
        cs = slice(part * width, (part + 1) * width)
        cg = conv(0, cs, cwg_ref, cbg_ref)
        cv = conv(1, cs, cwv_ref, cbv_ref)
        act = (cg * jax.nn.sigmoid(cg) * cv).astype(BF16)
        o_ref[...] += _dot(act, wd_ref[cs, :])

    @pl.when(f == pl.num_programs(1) - 1)
    def _():
        _residual_rms_rows(o_ref, h_ref, gpost_ref)


def _ffn_layer(h, layer, g_pre, w_up, conv_w, conv_b, w_down, g_post, seq_len):
    t, d = h.shape
    depth, d_ff, _ = w_down.shape
    tm, tf = _tiles(seq_len).rows, _tiles(seq_len).cols
    nf = d_ff // tf
    assert d_ff % tf == 0 and conv_w.shape[1] == 3
    taps = conv_w.shape[1]
    row = lambda m, f: (m, 0)
    vec_spec = pl.BlockSpec((None, 1, d), lambda m, f: (layer, 0, 0))
    conv_b3 = conv_b.reshape(depth, 1, 2 * d_ff)
    return pl.pallas_call(
        functools.partial(_ffn_kernel, tiles_per_seq=seq_len // tm),
        grid=(t // tm, nf),
        in_specs=[
            pl.BlockSpec((tm, d), row),
            vec_spec,
            pl.BlockSpec((None, d, tf), lambda m, f: (layer, 0, f)),
            pl.BlockSpec((None, d, tf), lambda m, f: (layer, 0, f + nf)),
            pl.BlockSpec((None, taps, tf), lambda m, f: (layer, 0, f)),
            pl.BlockSpec((None, taps, tf), lambda m, f: (layer, 0, f + nf)),
            pl.BlockSpec((None, 1, tf), lambda m, f: (layer, 0, f)),
            pl.BlockSpec((None, 1, tf), lambda m, f: (layer, 0, f + nf)),
            pl.BlockSpec((None, tf, d), lambda m, f: (layer, f, 0)),
            vec_spec,
        ],
        out_specs=pl.BlockSpec((tm, d), row),
        out_shape=jax.ShapeDtypeStruct((t, d), F32),
        scratch_shapes=[
            pltpu.VMEM((tm, d), BF16),
            pltpu.VMEM((nf, 2, V7X_SUBLANES, tf), F32),
            pltpu.VMEM((2, tm + V7X_SUBLANES, tf), F32),
        ],
        compiler_params=_params(2),
        name="conv_ffn",
    )(h, g_pre.reshape(depth, 1, d), w_up, w_up, conv_w, conv_w, conv_b3, conv_b3, w_down,
      g_post.reshape(depth, 1, d))


def _qkv_kernel(h_ref, gq_ref, gkv_ref, wq_ref, wk_ref, wv_ref, q_ref, k_ref, vt_ref,
                xq_ref, xkv_ref):
    j = pl.program_id(1)

    @pl.when(j == 0)
    def _():
        gq, gkv = gq_ref[...], gkv_ref[...]
        for r0 in range(0, h_ref.shape[0], PRENORM_ROW_BLOCK):
            rows = slice(r0, r0 + PRENORM_ROW_BLOCK)
            x = h_ref[rows, :]
            xs = x * lax.rsqrt(jnp.mean(x * x, axis=-1, keepdims=True) + EPS)
            xq_ref[rows, :] = (xs * gq).astype(BF16)
            xkv_ref[rows, :] = (xs * gkv).astype(BF16)

    scale = LOG2_E * HEAD_DIM ** -0.5
    q = _dot(xq_ref[...], wq_ref[...]) * scale
    k = _dot(xkv_ref[...], wk_ref[...])
    v = _dot(xkv_ref[...], wv_ref[...])
    for hh in range(q.shape[1] // HEAD_DIM):
        hs = slice(hh * HEAD_DIM, (hh + 1) * HEAD_DIM)
        q_ref[hh] = q[:, hs].astype(BF16)
        k_ref[hh] = k[:, hs].astype(BF16)
        vt_ref[hh] = v[:, hs].T.astype(BF16)


def _qkv(h, g_q, g_kv, w_q, w_k, w_v, seq_len):
    t, d = h.shape
    d_b = w_q.shape[1]
    n_heads = d_b // HEAD_DIM
    tm, tn = _tiles(seq_len).rows, _tiles(seq_len).cols
    hpt = tn // HEAD_DIM
    w_spec = pl.BlockSpec((d, tn), lambda m, j: (0, j))
    vec_spec = pl.BlockSpec((1, d), lambda m, j: (0, 0))
    hd_spec = pl.BlockSpec((hpt, tm, HEAD_DIM), lambda m, j: (j, m, 0))
    return pl.pallas_call(
        _qkv_kernel,
        grid=(t // tm, d_b // tn),
        in_specs=[pl.BlockSpec((tm, d), lambda m, j: (m, 0)), vec_spec, vec_spec,
                  w_spec, w_spec, w_spec],
        out_specs=[hd_spec, hd_spec,
                   pl.BlockSpec((hpt, HEAD_DIM, tm), lambda m, j: (j, 0, m))],
        out_shape=[jax.ShapeDtypeStruct((n_heads, t, HEAD_DIM), BF16),
                   jax.ShapeDtypeStruct((n_heads, t, HEAD_DIM), BF16),
                   jax.ShapeDtypeStruct((n_heads, HEAD_DIM, t), BF16)],
        scratch_shapes=[pltpu.VMEM((tm, d), BF16), pltpu.VMEM((tm, d), BF16)],
        compiler_params=_params(2),
        name="qkv_proj",
    )(h, g_q.reshape(1, d), g_kv.reshape(1, d), w_q, w_k, w_v)


def _attn_kernel(q_ref, k_ref, vt_ref, later_ref, o_ref, acc_ref, *, tq, hp):
    i = pl.program_id(2)
    key = lax.broadcasted_iota(jnp.int32, (tq, tq), 0)
    qry = lax.broadcasted_iota(jnp.int32, (tq, tq), 1)
    before_query = key < qry

    def scores(hh, jb, diagonal):
        start = pl.multiple_of(jb * tq, tq)
        ks = k_ref[hh, pl.ds(start, tq), :]
        z = lax.dot_general(ks, q_ref[hh], (((1,), (1,)), ((), ())),
                            preferred_element_type=F32)
        m = jnp.maximum(z, 0.0)
        zm = z - m
        l = jnp.log(1.0 + jnp.exp2(zm - m)) * LOG2_E
        sp = m + l
        if diagonal:
            sp = jnp.where(before_query, sp, 0.0)
        sp_hi = sp.astype(BF16)
        sp_lo = (sp - sp_hi.astype(F32)).astype(BF16)
        return jnp.concatenate([sp_hi, sp_lo], axis=0), z

    def weights(split, z, carry, diagonal):
        suffix = _dot(later_ref[...], split)
        a = jnp.exp2(z + carry - suffix)
        if diagonal:
            a = jnp.where(before_query, a, 0.0)
        return a.astype(BF16), carry - suffix[0:1]

    def values(hh, jb, a, first=False, keep=None):
        start = pl.multiple_of(jb * tq, tq)
        vt = vt_ref[hh, :, pl.ds(start, tq)]
        if keep is not None:
            vt = jnp.where(keep, vt, jnp.zeros_like(vt))
        pv = _dot(vt, a)
        if first:
            acc_ref[hh] = pv
        else:
            acc_ref[hh] += pv

    heads = range(hp)
    prev = jnp.maximum(i - 1, 0)
    sd = [scores(hh, i, True) for hh in heads]
    sv = [scores(hh, prev, False) for hh in heads]
    wd = [weights(*sd[hh], jnp.zeros((1, tq), F32), True) for hh in heads]
    wv = [weights(*sv[hh], wd[hh][1], False) for hh in heads]
    for hh in heads:
        values(hh, i, wd[hh][0], first=True)
    for hh in heads:
        values(hh, prev, wv[hh][0], keep=i > 0)
    carries = tuple(wv[hh][1] for hh in heads)

    least_used = functools.reduce(jnp.maximum, [jnp.max(c) for c in carries])
    remaining = jnp.where(least_used > -ATTN_LOG2_UNDERFLOW, i - 1, 0)

    def body(n, cs):
        jb = i - 2 - n
        s = [scores(hh, jb, False) for hh in heads]
        w = [weights(*s[hh], cs[hh], False) for hh in heads]
        for hh in heads:
            values(hh, jb, w[hh][0])
        return tuple(w[hh][1] for hh in heads)

    lax.fori_loop(0, remaining, body, carries)
    o_ref[...] = jnp.concatenate([acc_ref[hh].T for hh in heads], axis=1).astype(o_ref.dtype)


def _attention(q, k, vt, batch, seq_len):
    n_heads, t, dh = q.shape
    tq = _tiles(seq_len).attn
    hp = min(ATTN_HEADS_PER_STEP, n_heads)
    assert n_heads % hp == 0
    nq = seq_len // tq
    tri = jnp.triu(jnp.ones((tq, tq), BF16))
    later = jnp.concatenate([tri, tri], axis=1)
    return pl.pallas_call(
        functools.partial(_attn_kernel, tq=tq, hp=hp),
        grid=(batch, n_heads // hp, nq),
        in_specs=[
            pl.BlockSpec((hp, tq, dh), lambda b, g, i: (g, b * nq + i, 0)),
            pl.BlockSpec((hp, seq_len, dh), lambda b, g, i: (g, b, 0)),
            pl.BlockSpec((hp, dh, seq_len), lambda b, g, i: (g, 0, b)),
            pl.BlockSpec((tq, 2 * tq), lambda b, g, i: (0, 0)),
        ],
        out_specs=pl.BlockSpec((tq, hp * dh), lambda b, g, i: (b * nq + i, g)),
        out_shape=jax.ShapeDtypeStruct((t, n_heads * dh), BF16),
        scratch_shapes=[pltpu.VMEM((hp, dh, tq), F32)],
        compiler_params=_params(3),
        name="stickbreak_attn",
    )(q, k, vt, later)


def _proj_kernel(a_ref, w_ref, h_ref, g_ref, o_ref):
    o_ref[...] = h_ref[...] + _rms(_dot(a_ref[...], w_ref[...]), g_ref[...])


def _proj_residual(a, w, h, g, seq_len):
    t, d = h.shape
    d_in = a.shape[1]
    tm = _tiles(seq_len).proj_rows
    return pl.pallas_call(
        _proj_kernel,
        grid=(t // tm,),
        in_specs=[pl.BlockSpec((tm, d_in), lambda m: (m, 0)),
                  pl.BlockSpec((d_in, d), lambda m: (0, 0)),
                  pl.BlockSpec((tm, d), lambda m: (m, 0)),
                  pl.BlockSpec((1, d), lambda m: (0, 0))],
        out_specs=pl.BlockSpec((tm, d), lambda m: (m, 0)),
        out_shape=jax.ShapeDtypeStruct((t, d), F32),
        compiler_params=_params(1),
        name="attn_out_proj",
    )(a, w, h, g.reshape(1, d))


def kernel(x, pre_mix_g, post_mix_g, pre_ffn_g, post_ffn_g, a_w_in, a_v_norm_g, a_w_spatial, a_b_spatial, a_w_out, kv_norm_g, w_k, w_v, b_w_q, b_w_o, ffn_w_up, ffn_conv_w, ffn_conv_b, ffn_w_down):
    batch, seq_len, d = x.shape
    n_a = a_w_in.shape[0]
    n_b = b_w_q.shape[0]
    assert n_a == 1 and n_b == 1 and pre_mix_g.shape[0] == n_a + n_b
    assert a_w_spatial.shape[2] == CHUNK and w_k.shape[1] % HEAD_DIM == 0
    h = x.reshape(batch * seq_len, d)

    h = _gmlp_layer(h, pre_mix_g[0], a_w_in[0].astype(BF16), a_v_norm_g[0], a_w_spatial[0],
                    a_b_spatial[0], a_w_out[0].astype(BF16), post_mix_g[0], seq_len)
    w_up, w_down = ffn_w_up.astype(BF16), ffn_w_down.astype(BF16)
    ffn = functools.partial(_ffn_layer, g_pre=pre_ffn_g, w_up=w_up, conv_w=ffn_conv_w,
                            conv_b=ffn_conv_b, w_down=w_down, g_post=post_ffn_g, seq_len=seq_len)
    h = ffn(h, 0)

    q, k, vt = _qkv(h, pre_mix_g[1], kv_norm_g, b_w_q[0].astype(BF16), w_k.astype(BF16),
                    w_v.astype(BF16), seq_len)
    att = _attention(q, k, vt, batch, seq_len)
    h = _proj_residual(att, b_w_o[0].astype(BF16), h, post_mix_g[1], seq_len)
    h = ffn(h, 1)
    return h.reshape(batch, seq_len, d)
```

```python
import functools
from typing import NamedTuple

import jax
import jax.numpy as jnp
from jax import lax
from jax.experimental import pallas as pl
from jax.experimental.pallas import tpu as pltpu

CHUNK = 128
HEAD_DIM = 128
EPS = 1e-6
LOG2_E = 1.4426950408889634
ATTN_HEADS_PER_STEP = 8
DOWN_PROJ_SPLITS = 2
ATTN_LOG2_UNDERFLOW = 160.0

V7X_LANES = 128
V7X_SUBLANES = 8
V7X_VMEM_LIMIT_BYTES = 60000 * 1024

NORM_ROW_BLOCK = V7X_SUBLANES
PRENORM_ROW_BLOCK = 2 * V7X_SUBLANES

F32 = jnp.float32
BF16 = jnp.bfloat16


class _Tiles(NamedTuple):
    rows: int
    proj_rows: int
    cols: int
    attn: int


def _tiles(seq_len):
    tiles = _Tiles(rows=1024, proj_rows=512, cols=512, attn=256)
    assert all(seq_len % n == 0 for n in (tiles.rows, tiles.proj_rows, tiles.attn))
    assert tiles.rows % CHUNK == 0
    return tiles


def _params(n_axes):
    return pltpu.CompilerParams(
        dimension_semantics=("arbitrary",) * n_axes,
        vmem_limit_bytes=V7X_VMEM_LIMIT_BYTES,
    )


def _rms(x, g):
    ms = jnp.mean(x * x, axis=-1, keepdims=True)
    return x * lax.rsqrt(ms + EPS) * g


def _residual_rms_rows(o_ref, h_ref, g_ref):
    g = g_ref[...]
    for r0 in range(0, o_ref.shape[0], NORM_ROW_BLOCK):
        rows = slice(r0, r0 + NORM_ROW_BLOCK)
        o_ref[rows, :] = h_ref[rows, :] + _rms(o_ref[rows, :], g)


def _rms_rows(dst_ref, x_ref, g_ref):
    g = g_ref[...]
    for r0 in range(0, x_ref.shape[0], PRENORM_ROW_BLOCK):
        rows = slice(r0, r0 + PRENORM_ROW_BLOCK)
        dst_ref[rows, :] = _rms(x_ref[rows, :], g).astype(dst_ref.dtype)


def _first_middle_last(i, n, step, prologue, epilogue):
    def first():
        prologue()
        step(True)
        if n == 1:
            epilogue()

    def last():
        step(False)
        epilogue()

    pl.when(i == 0)(first)
    if n > 2:
        pl.when(jnp.logical_and(i > 0, i < n - 1))(functools.partial(step, False))
    if n > 1:
        pl.when(i == n - 1)(last)


def _dot(a, b):
    return jnp.dot(a, b, preferred_element_type=F32)


def _gmlp_kernel(h_ref, gpre_ref, wu_ref, wv_ref, vg_ref, ws_ref, bs_ref, wo_ref,
                 gpost_ref, o_ref, xn_ref, *, n_cols):
    j = pl.program_id(1)

    def step(first):
        xn = xn_ref[...]
        u = jax.nn.gelu(_dot(xn, wu_ref[...]))
        v = jax.nn.gelu(_dot(xn, wv_ref[...]))
        tm, tn = u.shape
        row = lax.broadcasted_iota(jnp.int32, (CHUNK, CHUNK), 0)
        col = lax.broadcasted_iota(jnp.int32, (CHUNK, CHUNK), 1)
        causal = row >= col
        per_part = tn // CHUNK // DOWN_PROJ_SPLITS
        for part in range(DOWN_PROJ_SPLITS):
            cols = []
            for g in range(part * per_part, (part + 1) * per_part):
                gs = slice(g * CHUNK, (g + 1) * CHUNK)
                vn = _rms(v[:, gs], vg_ref[:, gs]).astype(BF16)
                w = jnp.where(causal, ws_ref[g], 0.0).astype(BF16)
                b = bs_ref[g]
                rows = []
                for c in range(tm // CHUNK):
                    cs = slice(c * CHUNK, (c + 1) * CHUNK)
                    mixed = _dot(w, vn[cs, :]) + b
                    rows.append(u[cs, gs] * mixed)
                cols.append(jnp.concatenate(rows, axis=0))
            gated = jnp.concatenate(cols, axis=1).astype(BF16)
            ks = slice(part * per_part * CHUNK, (part + 1) * per_part * CHUNK)
            down = _dot(gated, wo_ref[ks, :])
            if first and part == 0:
                o_ref[...] = down
            else:
                o_ref[...] += down

    _first_middle_last(j, n_cols, step,
                       prologue=lambda: _rms_rows(xn_ref, h_ref, gpre_ref),
                       epilogue=lambda: _residual_rms_rows(o_ref, h_ref, gpost_ref))


def _gmlp_layer(h, g_pre, w_in, v_g, w_s, b_s, w_out, g_post, seq_len):
    t, d = h.shape
    d_a = w_out.shape[0]
    tm, tn = _tiles(seq_len).rows, _tiles(seq_len).cols
    nj = d_a // tn
    gpt = tn // CHUNK
    row = lambda m, j: (m, 0)
    vec_spec = pl.BlockSpec((1, d), lambda m, j: (0, 0))
    return pl.pallas_call(
        functools.partial(_gmlp_kernel, n_cols=nj),
        grid=(t // tm, nj),
        in_specs=[
            pl.BlockSpec((tm, d), row),
            vec_spec,
            pl.BlockSpec((d, tn), lambda m, j: (0, j)),
            pl.BlockSpec((d, tn), lambda m, j: (0, j + nj)),
            pl.BlockSpec((1, tn), lambda m, j: (0, j)),
            pl.BlockSpec((gpt, CHUNK, CHUNK), lambda m, j: (j, 0, 0)),
            pl.BlockSpec((gpt, CHUNK, 1), lambda m, j: (j, 0, 0)),
            pl.BlockSpec((tn, d), lambda m, j: (j, 0)),
            vec_spec,
        ],
        out_specs=pl.BlockSpec((tm, d), row),
        out_shape=jax.ShapeDtypeStruct((t, d), F32),
        scratch_shapes=[pltpu.VMEM((tm, d), BF16)],
        compiler_params=_params(2),
        name="gmlp_mixer",
    )(h, g_pre.reshape(1, d), w_in, w_in, v_g.reshape(1, d_a), w_s,
      b_s.reshape(b_s.shape[0], CHUNK, 1), w_out, g_post.reshape(1, d))


def _ffn_kernel(h_ref, gpre_ref, wg_ref, wv_ref, cwg_ref, cwv_ref, cbg_ref, cbv_ref,
                wd_ref, gpost_ref, o_ref, xn_ref, carry_ref, ext_ref, *, tiles_per_seq, n_cols):
    m = pl.program_id(0)
    f = pl.program_id(1)

    @pl.when(m % tiles_per_seq == 0)
    def _():
        carry_ref[f] = jnp.zeros(carry_ref.shape[1:], F32)

    def step(first):
        xn = xn_ref[...]
        tm = xn.shape[0]
        pad = V7X_SUBLANES
        ext_ref[:, 0:pad, :] = carry_ref[f]
        ext_ref[0, pad:, :] = _dot(xn, wg_ref[...])
        ext_ref[1, pad:, :] = _dot(xn, wv_ref[...])
        carry_ref[f] = ext_ref[:, tm:, :]

        def conv(which, cs, cw_ref, cb_ref):
            cw = cw_ref[:, cs]
            taps = [ext_ref[which, pad - s:pad - s + tm, cs] for s in (2, 1, 0)]
            return cb_ref[:, cs] + cw[0:1] * taps[0] + cw[1:2] * taps[1] + cw[2:3] * taps[2]

        width = wd_ref.shape[0] // DOWN_PROJ_SPLITS
        for part in range(DOWN_PROJ_SPLITS):
            cs = slice(part * width, (part + 1) * width)
            cg = conv(0, cs, cwg_ref, cbg_ref)
            cv = conv(1, cs, cwv_ref, cbv_ref)
            act = (cg * jax.nn.sigmoid(cg) * cv).astype(BF16)
            down = _dot(act, wd_ref[cs, :])
            if first and part == 0:
                o_ref[...] = down
            else:
                o_ref[...] += down

    _first_middle_last(f, n_cols, step,
                       prologue=lambda: _rms_rows(xn_ref, h_ref, gpre_ref),
                       epilogue=lambda: _residual_rms_rows(o_ref, h_ref, gpost_ref))


def _ffn_layer(h, layer, g_pre, w_up, conv_w, conv_b, w_down, g_post, seq_len):
    t, d = h.shape
    depth, d_ff, _ = w_down.shape
    tm, tf = _tiles(seq_len).rows, _tiles(seq_len).cols
    nf = d_ff // tf
    assert d_ff % tf == 0 and conv_w.shape[1] == 3
    taps = conv_w.shape[1]
    row = lambda m, f: (m, 0)
    vec_spec = pl.BlockSpec((None, 1, d), lambda m, f: (layer, 0, 0))
    conv_b3 = conv_b.reshape(depth, 1, 2 * d_ff)
    return pl.pallas_call(
        functools.partial(_ffn_kernel, tiles_per_seq=seq_len // tm, n_cols=nf),
        grid=(t // tm, nf),
        in_specs=[
            pl.BlockSpec((tm, d), row),
            vec_spec,
            pl.BlockSpec((None, d, tf), lambda m, f: (layer, 0, f)),
            pl.BlockSpec((None, d, tf), lambda m, f: (layer, 0, f + nf)),
            pl.BlockSpec((None, taps, tf), lambda m, f: (layer, 0, f)),
            pl.BlockSpec((None, taps, tf), lambda m, f: (layer, 0, f + nf)),
            pl.BlockSpec((None, 1, tf), lambda m, f: (layer, 0, f)),
            pl.BlockSpec((None, 1, tf), lambda m, f: (layer, 0, f + nf)),
            pl.BlockSpec((None, tf, d), lambda m, f: (layer, f, 0)),
            vec_spec,
        ],
        out_specs=pl.BlockSpec((tm, d), row),
        out_shape=jax.ShapeDtypeStruct((t, d), F32),
        scratch_shapes=[
            pltpu.VMEM((tm, d), BF16),
            pltpu.VMEM((nf, 2, V7X_SUBLANES, tf), F32),
            pltpu.VMEM((2, tm + V7X_SUBLANES, tf), F32),
        ],
        compiler_params=_params(2),
        name="conv_ffn",
    )(h, g_pre.reshape(depth, 1, d), w_up, w_up, conv_w, conv_w, conv_b3, conv_b3, w_down,
      g_post.reshape(depth, 1, d))


def _qkv_kernel(h_ref, gq_ref, gkv_ref, wq_ref, wk_ref, wv_ref, q_ref, k_ref, vt_ref,
                xq_ref, xkv_ref):
    j = pl.program_id(1)

    def normalize():
        gq, gkv = gq_ref[...], gkv_ref[...]
        for r0 in range(0, h_ref.shape[0], PRENORM_ROW_BLOCK):
            rows = slice(r0, r0 + PRENORM_ROW_BLOCK)
            x = h_ref[rows, :]
            xs = x * lax.rsqrt(jnp.mean(x * x, axis=-1, keepdims=True) + EPS)
            xq_ref[rows, :] = (xs * gq).astype(BF16)
            xkv_ref[rows, :] = (xs * gkv).astype(BF16)

    def project():
        scale = LOG2_E * HEAD_DIM ** -0.5
        q = _dot(xq_ref[...], wq_ref[...]) * scale
        k = _dot(xkv_ref[...], wk_ref[...])
        v = _dot(xkv_ref[...], wv_ref[...])
        for hh in range(q.shape[1] // HEAD_DIM):
            hs = slice(hh * HEAD_DIM, (hh + 1) * HEAD_DIM)
            q_ref[hh] = q[:, hs].astype(BF16)
            k_ref[hh] = k[:, hs].astype(BF16)
            vt_ref[hh] = v[:, hs].T.astype(BF16)

    def first_step():
        normalize()
        project()

    pl.when(j == 0)(first_step)
    pl.when(j > 0)(project)


def _qkv(h, g_q, g_kv, w_q, w_k, w_v, seq_len):
    t, d = h.shape
    d_b = w_q.shape[1]
    n_heads = d_b // HEAD_DIM
    tm, tn = _tiles(seq_len).rows, _tiles(seq_len).cols
    hpt = tn // HEAD_DIM
    w_spec = pl.BlockSpec((d, tn), lambda m, j: (0, j))
    vec_spec = pl.BlockSpec((1, d), lambda m, j: (0, 0))
    hd_spec = pl.BlockSpec((hpt, tm, HEAD_DIM), lambda m, j: (j, m, 0))
    return pl.pallas_call(
        _qkv_kernel,
        grid=(t // tm, d_b // tn),
        in_specs=[pl.BlockSpec((tm, d), lambda m, j: (m, 0)), vec_spec, vec_spec,
                  w_spec, w_spec, w_spec],
        out_specs=[hd_spec, hd_spec,
                   pl.BlockSpec((hpt, HEAD_DIM, tm), lambda m, j: (j, 0, m))],
        out_shape=[jax.ShapeDtypeStruct((n_heads, t, HEAD_DIM), BF16),
                   jax.ShapeDtypeStruct((n_heads, t, HEAD_DIM), BF16),
                   jax.ShapeDtypeStruct((n_heads, HEAD_DIM, t), BF16)],
        scratch_shapes=[pltpu.VMEM((tm, d), BF16), pltpu.VMEM((tm, d), BF16)],
        compiler_params=_params(2),
        name="qkv_proj",
    )(h, g_q.reshape(1, d), g_kv.reshape(1, d), w_q, w_k, w_v)


def _attn_kernel(q_ref, k_ref, vt_ref, later_ref, o_ref, acc_ref, *, tq, hp):
    i = pl.program_id(2)
    key = lax.broadcasted_iota(jnp.int32, (tq, tq), 0)
    qry = lax.broadcasted_iota(jnp.int32, (tq, tq), 1)
    before_query = key < qry

    def scores(hh, jb, diagonal):
        start = pl.multiple_of(jb * tq, tq)
        ks = k_ref[hh, pl.ds(start, tq), :]
        z = lax.dot_general(ks, q_ref[hh], (((1,), (1,)), ((), ())),
                            preferred_element_type=F32)
        m = jnp.maximum(z, 0.0)
        zm = z - m
        l = jnp.log(1.0 + jnp.exp2(zm - m)) * LOG2_E
        sp = m + l
        if diagonal:
            sp = jnp.where(before_query, sp, 0.0)
        sp_hi = sp.astype(BF16)
        sp_lo = (sp - sp_hi.astype(F32)).astype(BF16)
        return jnp.concatenate([sp_hi, sp_lo], axis=0), z

    def weights(split, z, carry, diagonal):
        suffix = _dot(later_ref[...], split)
        a = jnp.exp2(z + carry - suffix)
        if diagonal:
            a = jnp.where(before_query, a, 0.0)
        return a.astype(BF16), carry - suffix[0:1]

    def values(hh, jb, a, first=False, keep=None):
        start = pl.multiple_of(jb * tq, tq)
        vt = vt_ref[hh, :, pl.ds(start, tq)]
        if keep is not None:
            vt = jnp.where(keep, vt, jnp.zeros_like(vt))
        pv = _dot(vt, a)
        if first:
            acc_ref[hh] = pv
        else:
            acc_ref[hh] += pv

    heads = range(hp)
    prev = jnp.maximum(i - 1, 0)
    sd = [scores(hh, i, True) for hh in heads]
    sv = [scores(hh, prev, False) for hh in heads]
    wd = [weights(*sd[hh], jnp.zeros((1, tq), F32), True) for hh in heads]
    wv = [weights(*sv[hh], wd[hh][1], False) for hh in heads]
    for hh in heads:
        values(hh, i, wd[hh][0], first=True)
    for hh in heads:
        values(hh, prev, wv[hh][0], keep=i > 0)
    carries = tuple(wv[hh][1] for hh in heads)

    least_used = functools.reduce(jnp.maximum, [jnp.max(c) for c in carries])
    remaining = jnp.where(least_used > -ATTN_LOG2_UNDERFLOW, i - 1, 0)

    def body(n, cs):
        jb = i - 2 - n
        s = [scores(hh, jb, False) for hh in heads]
        w = [weights(*s[hh], cs[hh], False) for hh in heads]
        for hh in heads:
            values(hh, jb, w[hh][0])
        return tuple(w[hh][1] for hh in heads)

    lax.fori_loop(0, remaining, body, carries)
    o_ref[...] = jnp.concatenate([acc_ref[hh].T for hh in heads], axis=1).astype(o_ref.dtype)


def _attention(q, k, vt, batch, seq_len):
    n_heads, t, dh = q.shape
    tq = _tiles(seq_len).attn
    hp = min(ATTN_HEADS_PER_STEP, n_heads)
    assert n_heads % hp == 0
    nq = seq_len // tq
    tri = jnp.triu(jnp.ones((tq, tq), BF16))
    later = jnp.concatenate([tri, tri], axis=1)
    return pl.pallas_call(
        functools.partial(_attn_kernel, tq=tq, hp=hp),
        grid=(batch, n_heads // hp, nq),
        in_specs=[
            pl.BlockSpec((hp, tq, dh), lambda b, g, i: (g, b * nq + i, 0)),
            pl.BlockSpec((hp, seq_len, dh), lambda b, g, i: (g, b, 0)),
            pl.BlockSpec((hp, dh, seq_len), lambda b, g, i: (g, 0, b)),
            pl.BlockSpec((tq, 2 * tq), lambda b, g, i: (0, 0)),
        ],
        out_specs=pl.BlockSpec((tq, hp * dh), lambda b, g, i: (b * nq + i, g)),
        out_shape=jax.ShapeDtypeStruct((t, n_heads * dh), BF16),
        scratch_shapes=[pltpu.VMEM((hp, dh, tq), F32)],
        compiler_params=_params(3),
        name="stickbreak_attn",
    )(q, k, vt, later)


def _proj_kernel(a_ref, w_ref, h_ref, g_ref, o_ref):
    o_ref[...] = h_ref[...] + _rms(_dot(a_ref[...], w_ref[...]), g_ref[...])


def _proj_residual(a, w, h, g, seq_len):
    t, d = h.shape
    d_in = a.shape[1]
    tm = _tiles(seq_len).proj_rows
    return pl.pallas_call(
        _proj_kernel,
        grid=(t // tm,),
        in_specs=[pl.BlockSpec((tm, d_in), lambda m: (m, 0)),
                  pl.BlockSpec((d_in, d), lambda m: (0, 0)),
                  pl.BlockSpec((tm, d), lambda m: (m, 0)),
                  pl.BlockSpec((1, d), lambda m: (0, 0))],
        out_specs=pl.BlockSpec((tm, d), lambda m: (m, 0)),
        out_shape=jax.ShapeDtypeStruct((t, d), F32),
        compiler_params=_params(1),
        name="attn_out_proj",
    )(a, w, h, g.reshape(1, d))


def kernel(x, pre_mix_g, post_mix_g, pre_ffn_g, post_ffn_g, a_w_in, a_v_norm_g, a_w_spatial, a_b_spatial, a_w_out, kv_norm_g, w_k, w_v, b_w_q, b_w_o, ffn_w_up, ffn_conv_w, ffn_conv_b, ffn_w_down):
    batch, seq_len, d = x.shape
    n_a = a_w_in.shape[0]
    n_b = b_w_q.shape[0]
    assert n_a == 1 and n_b == 1 and pre_mix_g.shape[0] == n_a + n_b
    assert a_w_spatial.shape[2] == CHUNK and w_k.shape[1] % HEAD_DIM == 0
    h = x.reshape(batch * seq_len, d)

    h = _gmlp_layer(h, pre_mix_g[0], a_w_in[0].astype(BF16), a_v_norm_g[0], a_w_spatial[0],
                    a_b_spatial[0], a_w_out[0].astype(BF16), post_mix_g[0], seq_len)
    w_up, w_down = ffn_w_up.astype(BF16), ffn_w_down.astype(BF16)
    ffn = functools.partial(_ffn_layer, g_pre=pre_ffn_g, w_up=w_up, conv_w=ffn_conv_w,
                            conv_b=ffn_conv_b, w_down=w_down, g_post=post_ffn_g, seq_len=seq_len)
    h = ffn(h, 0)

    q, k, vt = _qkv(h, pre_mix_g[1], kv_norm_g, b_w_q[0].astype(BF16), w_k.astype(BF16),
                    w_v.astype(BF16), seq_len)
    att = _attention(q, k, vt, batch, seq_len)
    h = _proj_residual(att, b_w_o[0].astype(BF16), h, post_mix_g[1], seq_len)
    h = ffn(h, 1)
    return h.reshape(batch, seq_len, d)
```

```python
import functools
from typing import NamedTuple

import jax
import jax.numpy as jnp
from jax import lax
from jax.experimental import pallas as pl
from jax.experimental.pallas import tpu as pltpu

CHUNK = 128
HEAD_DIM = 128
EPS = 1e-6
LOG2_E = 1.4426950408889634
ATTN_HEADS_PER_STEP = 8
DOWN_PROJ_SPLITS = 2
ATTN_LOG2_UNDERFLOW = 160.0

V7X_LANES = 128
V7X_SUBLANES = 8
V7X_VMEM_LIMIT_BYTES = 60000 * 1024

NORM_ROW_BLOCK = V7X_SUBLANES
PRENORM_ROW_BLOCK = 2 * V7X_SUBLANES

F32 = jnp.float32
BF16 = jnp.bfloat16


class _Tiles(NamedTuple):
    rows: int
    proj_rows: int
    cols: int
    attn: int


def _tiles(seq_len):
    tiles = _Tiles(rows=1024, proj_rows=512, cols=512, attn=256)
    assert all(seq_len % n == 0 for n in (tiles.rows, tiles.proj_rows, tiles.attn))
    assert tiles.rows % CHUNK == 0
    return tiles


def _params(n_axes):
    return pltpu.CompilerParams(
        dimension_semantics=("arbitrary",) * n_axes,
        vmem_limit_bytes=V7X_VMEM_LIMIT_BYTES,
    )


def _rms(x, g):
    ms = jnp.mean(x * x, axis=-1, keepdims=True)
    return x * lax.rsqrt(ms + EPS) * g


def _residual_rms_rows(o_ref, h_ref, g_ref):
    g = g_ref[...]
    for r0 in range(0, o_ref.shape[0], NORM_ROW_BLOCK):
        rows = slice(r0, r0 + NORM_ROW_BLOCK)
        o_ref[rows, :] = h_ref[rows, :] + _rms(o_ref[rows, :], g)


def _rms_rows(dst_ref, x_ref, g_ref):
    g = g_ref[...]
    for r0 in range(0, x_ref.shape[0], PRENORM_ROW_BLOCK):
        rows = slice(r0, r0 + PRENORM_ROW_BLOCK)
        dst_ref[rows, :] = _rms(x_ref[rows, :], g).astype(dst_ref.dtype)


def _first_rest(i, n, step, prologue, epilogue):
    def first():
        prologue()
        step(True)

    pl.when(i == 0)(first)
    if n > 1:
        pl.when(i > 0)(functools.partial(step, False))
    pl.when(i == n - 1)(epilogue)


def _dot(a, b):
    return jnp.dot(a, b, preferred_element_type=F32)


def _gmlp_kernel(h_ref, gpre_ref, wu_ref, wv_ref, vg_ref, ws_ref, bs_ref, wo_ref,
                 gpost_ref, o_ref, xn_ref, *, n_cols):
    j = pl.program_id(1)

    def step(first):
        xn = xn_ref[...]
        u = jax.nn.gelu(_dot(xn, wu_ref[...]))
        v = jax.nn.gelu(_dot(xn, wv_ref[...]))
        tm, tn = u.shape
        row = lax.broadcasted_iota(jnp.int32, (CHUNK, CHUNK), 0)
        col = lax.broadcasted_iota(jnp.int32, (CHUNK, CHUNK), 1)
        causal = row >= col
        per_part = tn // CHUNK // DOWN_PROJ_SPLITS
        for part in range(DOWN_PROJ_SPLITS):
            cols = []
            for g in range(part * per_part, (part + 1) * per_part):
                gs = slice(g * CHUNK, (g + 1) * CHUNK)
                vn = _rms(v[:, gs], vg_ref[:, gs]).astype(BF16)
                w = jnp.where(causal, ws_ref[g], 0.0).astype(BF16)
                b = bs_ref[g]
                rows = []
                for c in range(tm // CHUNK):
                    cs = slice(c * CHUNK, (c + 1) * CHUNK)
                    mixed = _dot(w, vn[cs, :]) + b
                    rows.append(u[cs, gs] * mixed)
                cols.append(jnp.concatenate(rows, axis=0))
            gated = jnp.concatenate(cols, axis=1).astype(BF16)
            ks = slice(part * per_part * CHUNK, (part + 1) * per_part * CHUNK)
            down = _dot(gated, wo_ref[ks, :])
            if first and part == 0:
                o_ref[...] = down
            else:
                o_ref[...] += down

    _first_rest(j, n_cols, step,
                       prologue=lambda: _rms_rows(xn_ref, h_ref, gpre_ref),
                       epilogue=lambda: _residual_rms_rows(o_ref, h_ref, gpost_ref))


def _gmlp_layer(h, g_pre, w_in, v_g, w_s, b_s, w_out, g_post, seq_len):
    t, d = h.shape
    d_a = w_out.shape[0]
    tm, tn = _tiles(seq_len).rows, _tiles(seq_len).cols
    nj = d_a // tn
    gpt = tn // CHUNK
    row = lambda m, j: (m, 0)
    vec_spec = pl.BlockSpec((1, d), lambda m, j: (0, 0))
    return pl.pallas_call(
        functools.partial(_gmlp_kernel, n_cols=nj),
        grid=(t // tm, nj),
        in_specs=[
            pl.BlockSpec((tm, d), row),
            vec_spec,
            pl.BlockSpec((d, tn), lambda m, j: (0, j)),
            pl.BlockSpec((d, tn), lambda m, j: (0, j + nj)),
            pl.BlockSpec((1, tn), lambda m, j: (0, j)),
            pl.BlockSpec((gpt, CHUNK, CHUNK), lambda m, j: (j, 0, 0)),
            pl.BlockSpec((gpt, CHUNK, 1), lambda m, j: (j, 0, 0)),
            pl.BlockSpec((tn, d), lambda m, j: (j, 0)),
            vec_spec,
        ],
        out_specs=pl.BlockSpec((tm, d), row),
        out_shape=jax.ShapeDtypeStruct((t, d), F32),
        scratch_shapes=[pltpu.VMEM((tm, d), BF16)],
        compiler_params=_params(2),
        name="gmlp_mixer",
    )(h, g_pre.reshape(1, d), w_in, w_in, v_g.reshape(1, d_a), w_s,
      b_s.reshape(b_s.shape[0], CHUNK, 1), w_out, g_post.reshape(1, d))


def _ffn_kernel(h_ref, gpre_ref, wg_ref, wv_ref, cwg_ref, cwv_ref, cbg_ref, cbv_ref,
                wd_ref, gpost_ref, o_ref, xn_ref, carry_ref, ext_ref, *, tiles_per_seq, n_cols):
    m = pl.program_id(0)
    f = pl.program_id(1)

    @pl.when(m % tiles_per_seq == 0)
    def _():
        carry_ref[f] = jnp.zeros(carry_ref.shape[1:], F32)

    def step(first):
        xn = xn_ref[...]
        tm = xn.shape[0]
        pad = V7X_SUBLANES
        ext_ref[:, 0:pad, :] = carry_ref[f]
        ext_ref[0, pad:, :] = _dot(xn, wg_ref[...])
        ext_ref[1, pad:, :] = _dot(xn, wv_ref[...])
        carry_ref[f] = ext_ref[:, tm:, :]

        def conv(which, cs, cw_ref, cb_ref):
            cw = cw_ref[:, cs]
            taps = [ext_ref[which, pad - s:pad - s + tm, cs] for s in (2, 1, 0)]
            return cb_ref[:, cs] + cw[0:1] * taps[0] + cw[1:2] * taps[1] + cw[2:3] * taps[2]

        width = wd_ref.shape[0] // DOWN_PROJ_SPLITS
        for part in range(DOWN_PROJ_SPLITS):
            cs = slice(part * width, (part + 1) * width)
            cg = conv(0, cs, cwg_ref, cbg_ref)
            cv = conv(1, cs, cwv_ref, cbv_ref)
            act = (cg * jax.nn.sigmoid(cg) * cv).astype(BF16)
            down = _dot(act, wd_ref[cs, :])
            if first and part == 0:
                o_ref[...] = down
            else:
                o_ref[...] += down

    _first_rest(f, n_cols, step,
                       prologue=lambda: _rms_rows(xn_ref, h_ref, gpre_ref),
                       epilogue=lambda: _residual_rms_rows(o_ref, h_ref, gpost_ref))


def _ffn_layer(h, layer, g_pre, w_up, conv_w, conv_b, w_down, g_post, seq_len):
    t, d = h.shape
    depth, d_ff, _ = w_down.shape
    tm, tf = _tiles(seq_len).rows, _tiles(seq_len).cols
    nf = d_ff // tf
    assert d_ff % tf == 0 and conv_w.shape[1] == 3
    taps = conv_w.shape[1]
    row = lambda m, f: (m, 0)
    vec_spec = pl.BlockSpec((None, 1, d), lambda m, f: (layer, 0, 0))
    conv_b3 = conv_b.reshape(depth, 1, 2 * d_ff)
    return pl.pallas_call(
        functools.partial(_ffn_kernel, tiles_per_seq=seq_len // tm, n_cols=nf),
        grid=(t // tm, nf),
        in_specs=[
            pl.BlockSpec((tm, d), row),
            vec_spec,
            pl.BlockSpec((None, d, tf), lambda m, f: (layer, 0, f)),
            pl.BlockSpec((None, d, tf), lambda m, f: (layer, 0, f + nf)),
            pl.BlockSpec((None, taps, tf), lambda m, f: (layer, 0, f)),
            pl.BlockSpec((None, taps, tf), lambda m, f: (layer, 0, f + nf)),
            pl.BlockSpec((None, 1, tf), lambda m, f: (layer, 0, f)),
            pl.BlockSpec((None, 1, tf), lambda m, f: (layer, 0, f + nf)),
            pl.BlockSpec((None, tf, d), lambda m, f: (layer, f, 0)),
            vec_spec,
        ],
        out_specs=pl.BlockSpec((tm, d), row),
        out_shape=jax.ShapeDtypeStruct((t, d), F32),
        scratch_shapes=[
            pltpu.VMEM((tm, d), BF16),
            pltpu.VMEM((nf, 2, V7X_SUBLANES, tf), F32),
            pltpu.VMEM((2, tm + V7X_SUBLANES, tf), F32),
        ],
        compiler_params=_params(2),
        name="conv_ffn",
    )(h, g_pre.reshape(depth, 1, d), w_up, w_up, conv_w, conv_w, conv_b3, conv_b3, w_down,
      g_post.reshape(depth, 1, d))


def _qkv_kernel(h_ref, gq_ref, gkv_ref, wq_ref, wk_ref, wv_ref, q_ref, k_ref, vt_ref,
                xq_ref, xkv_ref):
    j = pl.program_id(1)

    def normalize():
        gq, gkv = gq_ref[...], gkv_ref[...]
        for r0 in range(0, h_ref.shape[0], PRENORM_ROW_BLOCK):
            rows = slice(r0, r0 + PRENORM_ROW_BLOCK)
            x = h_ref[rows, :]
            xs = x * lax.rsqrt(jnp.mean(x * x, axis=-1, keepdims=True) + EPS)
            xq_ref[rows, :] = (xs * gq).astype(BF16)
            xkv_ref[rows, :] = (xs * gkv).astype(BF16)

    def project():
        scale = LOG2_E * HEAD_DIM ** -0.5
        q = _dot(xq_ref[...], wq_ref[...]) * scale
        k = _dot(xkv_ref[...], wk_ref[...])
        v = _dot(xkv_ref[...], wv_ref[...])
        for hh in range(q.shape[1] // HEAD_DIM):
            hs = slice(hh * HEAD_DIM, (hh + 1) * HEAD_DIM)
            q_ref[hh] = q[:, hs].astype(BF16)
            k_ref[hh] = k[:, hs].astype(BF16)
            vt_ref[hh] = v[:, hs].T.astype(BF16)

    def first_step():
        normalize()
        project()

    pl.when(j == 0)(first_step)
    pl.when(j > 0)(project)


def _qkv(h, g_q, g_kv, w_q, w_k, w_v, seq_len):
    t, d = h.shape
    d_b = w_q.shape[1]
    n_heads = d_b // HEAD_DIM
    tm, tn = _tiles(seq_len).rows, _tiles(seq_len).cols
    hpt = tn // HEAD_DIM
    w_spec = pl.BlockSpec((d, tn), lambda m, j: (0, j))
    vec_spec = pl.BlockSpec((1, d), lambda m, j: (0, 0))
    hd_spec = pl.BlockSpec((hpt, tm, HEAD_DIM), lambda m, j: (j, m, 0))
    return pl.pallas_call(
        _qkv_kernel,
        grid=(t // tm, d_b // tn),
        in_specs=[pl.BlockSpec((tm, d), lambda m, j: (m, 0)), vec_spec, vec_spec,
                  w_spec, w_spec, w_spec],
        out_specs=[hd_spec, hd_spec,
                   pl.BlockSpec((hpt, HEAD_DIM, tm), lambda m, j: (j, 0, m))],
        out_shape=[jax.ShapeDtypeStruct((n_heads, t, HEAD_DIM), BF16),
                   jax.ShapeDtypeStruct((n_heads, t, HEAD_DIM), BF16),
                   jax.ShapeDtypeStruct((n_heads, HEAD_DIM, t), BF16)],
        scratch_shapes=[pltpu.VMEM((tm, d), BF16), pltpu.VMEM((tm, d), BF16)],
        compiler_params=_params(2),
        name="qkv_proj",
    )(h, g_q.reshape(1, d), g_kv.reshape(1, d), w_q, w_k, w_v)


def _attn_kernel(q_ref, k_ref, vt_ref, later_ref, o_ref, acc_ref, *, tq, hp):
    i = pl.program_id(2)
    key = lax.broadcasted_iota(jnp.int32, (tq, tq), 0)
    qry = lax.broadcasted_iota(jnp.int32, (tq, tq), 1)
    before_query = key < qry

    def scores(hh, jb, diagonal):
        start = pl.multiple_of(jb * tq, tq)
        ks = k_ref[hh, pl.ds(start, tq), :]
        z = lax.dot_general(ks, q_ref[hh], (((1,), (1,)), ((), ())),
                            preferred_element_type=F32)
        m = jnp.maximum(z, 0.0)
        zm = z - m
        l = jnp.log(1.0 + jnp.exp2(zm - m)) * LOG2_E
        sp = m + l
        if diagonal:
            sp = jnp.where(before_query, sp, 0.0)
        sp_hi = sp.astype(BF16)
        sp_lo = (sp - sp_hi.astype(F32)).astype(BF16)
        return jnp.concatenate([sp_hi, sp_lo], axis=0), z

    def weights(split, z, carry, diagonal):
        suffix = _dot(later_ref[...], split)
        a = jnp.exp2(z + carry - suffix)
        if diagonal:
            a = jnp.where(before_query, a, 0.0)
        return a.astype(BF16), carry - suffix[0:1]

    def values(hh, jb, a, first=False, keep=None):
        start = pl.multiple_of(jb * tq, tq)
        vt = vt_ref[hh, :, pl.ds(start, tq)]
        if keep is not None:
            vt = jnp.where(keep, vt, jnp.zeros_like(vt))
        pv = _dot(vt, a)
        if first:
            acc_ref[hh] = pv
        else:
            acc_ref[hh] += pv

    heads = range(hp)
    prev = jnp.maximum(i - 1, 0)
    sd = [scores(hh, i, True) for hh in heads]
    sv = [scores(hh, prev, False) for hh in heads]
    wd = [weights(*sd[hh], jnp.zeros((1, tq), F32), True) for hh in heads]
    wv = [weights(*sv[hh], wd[hh][1], False) for hh in heads]
    for hh in heads:
        values(hh, i, wd[hh][0], first=True)
    for hh in heads:
        values(hh, prev, wv[hh][0], keep=i > 0)
    carries = tuple(wv[hh][1] for hh in heads)

    least_used = functools.reduce(jnp.maximum, [jnp.max(c) for c in carries])
    remaining = jnp.where(least_used > -ATTN_LOG2_UNDERFLOW, i - 1, 0)

    def body(n, cs):
        jb = i - 2 - n
        s = [scores(hh, jb, False) for hh in heads]
        w = [weights(*s[hh], cs[hh], False) for hh in heads]
        for hh in heads:
            values(hh, jb, w[hh][0])
        return tuple(w[hh][1] for hh in heads)

    lax.fori_loop(0, remaining, body, carries)
    o_ref[...] = jnp.concatenate([acc_ref[hh].T for hh in heads], axis=1).astype(o_ref.dtype)


def _attention(q, k, vt, batch, seq_len):
    n_heads, t, dh = q.shape
    tq = _tiles(seq_len).attn
    hp = min(ATTN_HEADS_PER_STEP, n_heads)
    assert n_heads % hp == 0
    nq = seq_len // tq
    tri = jnp.triu(jnp.ones((tq, tq), BF16))
    later = jnp.concatenate([tri, tri], axis=1)
    return pl.pallas_call(
        functools.partial(_attn_kernel, tq=tq, hp=hp),
        grid=(batch, n_heads // hp, nq),
        in_specs=[
            pl.BlockSpec((hp, tq, dh), lambda b, g, i: (g, b * nq + i, 0)),
            pl.BlockSpec((hp, seq_len, dh), lambda b, g, i: (g, b, 0)),
            pl.BlockSpec((hp, dh, seq_len), lambda b, g, i: (g, 0, b)),
            pl.BlockSpec((tq, 2 * tq), lambda b, g, i: (0, 0)),
        ],
        out_specs=pl.BlockSpec((tq, hp * dh), lambda b, g, i: (b * nq + i, g)),
        out_shape=jax.ShapeDtypeStruct((t, n_heads * dh), BF16),
        scratch_shapes=[pltpu.VMEM((hp, dh, tq), F32)],
        compiler_params=_params(3),
        name="stickbreak_attn",
    )(q, k, vt, later)


def _proj_kernel(a_ref, w_ref, h_ref, g_ref, o_ref):
    o_ref[...] = h_ref[...] + _rms(_dot(a_ref[...], w_ref[...]), g_ref[...])


def _proj_residual(a, w, h, g, seq_len):
    t, d = h.shape
    d_in = a.shape[1]
    tm = _tiles(seq_len).proj_rows
    return pl.pallas_call(
        _proj_kernel,
        grid=(t // tm,),
        in_specs=[pl.BlockSpec((tm, d_in), lambda m: (m, 0)),
                  pl.BlockSpec((d_in, d), lambda m: (0, 0)),
                  pl.BlockSpec((tm, d), lambda m: (m, 0)),
                  pl.BlockSpec((1, d), lambda m: (0, 0))],
        out_specs=pl.BlockSpec((tm, d), lambda m: (m, 0)),
        out_shape=jax.ShapeDtypeStruct((t, d), F32),
        compiler_params=_params(1),
        name="attn_out_proj",
    )(a, w, h, g.reshape(1, d))


def kernel(x, pre_mix_g, post_mix_g, pre_ffn_g, post_ffn_g, a_w_in, a_v_norm_g, a_w_spatial, a_b_spatial, a_w_out, kv_norm_g, w_k, w_v, b_w_q, b_w_o, ffn_w_up, ffn_conv_w, ffn_conv_b, ffn_w_down):
    batch, seq_len, d = x.shape
    n_a = a_w_in.shape[0]
    n_b = b_w_q.shape[0]
    assert n_a == 1 and n_b == 1 and pre_mix_g.shape[0] == n_a + n_b
    assert a_w_spatial.shape[2] == CHUNK and w_k.shape[1] % HEAD_DIM == 0
    h = x.reshape(batch * seq_len, d)

    h = _gmlp_layer(h, pre_mix_g[0], a_w_in[0].astype(BF16), a_v_norm_g[0], a_w_spatial[0],
                    a_b_spatial[0], a_w_out[0].astype(BF16), post_mix_g[0], seq_len)
    w_up, w_down = ffn_w_up.astype(BF16), ffn_w_down.astype(BF16)
    ffn = functools.partial(_ffn_layer, g_pre=pre_ffn_g, w_up=w_up, conv_w=ffn_conv_w,
                            conv_b=ffn_conv_b, w_down=w_down, g_post=post_ffn_g, seq_len=seq_len)
    h = ffn(h, 0)

    q, k, vt = _qkv(h, pre_mix_g[1], kv_norm_g, b_w_q[0].astype(BF16), w_k.astype(BF16),
                    w_v.astype(BF16), seq_len)
    att = _attention(q, k, vt, batch, seq_len)
    h = _proj_residual(att, b_w_o[0].astype(BF16), h, post_mix_g[1], seq_len)
    h = ffn(h, 1)
    return h.reshape(batch, seq_len, d)
```

```python
import functools
from typing import NamedTuple

import jax
import jax.numpy as jnp
from jax import lax
from jax.experimental import pallas as pl
from jax.experimental.pallas import tpu as pltpu

CHUNK = 128
HEAD_DIM = 128
EPS = 1e-6
LOG2_E = 1.4426950408889634
ATTN_HEADS_PER_STEP = 8
ATTN_QUERY_BLOCKS_PER_STEP = 2
DOWN_PROJ_SPLITS = 2
ATTN_LOG2_UNDERFLOW = 160.0

V7X_LANES = 128
V7X_SUBLANES = 8
V7X_VMEM_LIMIT_BYTES = 60000 * 1024

NORM_ROW_BLOCK = V7X_SUBLANES
PRENORM_ROW_BLOCK = 2 * V7X_SUBLANES

F32 = jnp.float32
BF16 = jnp.bfloat16


class _Tiles(NamedTuple):
    rows: int
    proj_rows: int
    cols: int
    attn: int


def _tiles(seq_len):
    tiles = _Tiles(rows=1024, proj_rows=512, cols=512, attn=256)
    assert all(seq_len % n == 0 for n in (tiles.rows, tiles.proj_rows, tiles.attn))
    assert tiles.rows % CHUNK == 0
    return tiles


def _params(n_axes):
    return pltpu.CompilerParams(
        dimension_semantics=("arbitrary",) * n_axes,
        vmem_limit_bytes=V7X_VMEM_LIMIT_BYTES,
    )


def _rms(x, g):
    ms = jnp.mean(x * x, axis=-1, keepdims=True)
    return x * lax.rsqrt(ms + EPS) * g


def _residual_rms_rows(o_ref, h_ref, g_ref):
    g = g_ref[...]
    for r0 in range(0, o_ref.shape[0], NORM_ROW_BLOCK):
        rows = slice(r0, r0 + NORM_ROW_BLOCK)
        o_ref[rows, :] = h_ref[rows, :] + _rms(o_ref[rows, :], g)


def _rms_rows(dst_ref, x_ref, g_ref):
    g = g_ref[...]
    for r0 in range(0, x_ref.shape[0], PRENORM_ROW_BLOCK):
        rows = slice(r0, r0 + PRENORM_ROW_BLOCK)
        dst_ref[rows, :] = _rms(x_ref[rows, :], g).astype(dst_ref.dtype)


def _first_rest(i, n, step, prologue, epilogue):
    def first():
        prologue()
        step(True)

    pl.when(i == 0)(first)
    if n > 1:
        pl.when(i > 0)(functools.partial(step, False))
    pl.when(i == n - 1)(epilogue)


def _dot(a, b):
    return jnp.dot(a, b, preferred_element_type=F32)


def _gmlp_kernel(h_ref, gpre_ref, wu_ref, wv_ref, vg_ref, ws_ref, bs_ref, wo_ref,
                 gpost_ref, o_ref, xn_ref, *, n_cols):
    j = pl.program_id(1)

    def step(first):
        xn = xn_ref[...]
        u = jax.nn.gelu(_dot(xn, wu_ref[...]))
        v = jax.nn.gelu(_dot(xn, wv_ref[...]))
        tm, tn = u.shape
        row = lax.broadcasted_iota(jnp.int32, (CHUNK, CHUNK), 0)
        col = lax.broadcasted_iota(jnp.int32, (CHUNK, CHUNK), 1)
        causal = row >= col
        per_part = tn // CHUNK // DOWN_PROJ_SPLITS
        for part in range(DOWN_PROJ_SPLITS):
            cols = []
            for g in range(part * per_part, (part + 1) * per_part):
                gs = slice(g * CHUNK, (g + 1) * CHUNK)
                vn = _rms(v[:, gs], vg_ref[:, gs]).astype(BF16)
                w = jnp.where(causal, ws_ref[g], 0.0).astype(BF16)
                b = bs_ref[g]
                rows = []
                for c in range(tm // CHUNK):
                    cs = slice(c * CHUNK, (c + 1) * CHUNK)
                    mixed = _dot(w, vn[cs, :]) + b
                    rows.append(u[cs, gs] * mixed)
                cols.append(jnp.concatenate(rows, axis=0))
            gated = jnp.concatenate(cols, axis=1).astype(BF16)
            ks = slice(part * per_part * CHUNK, (part + 1) * per_part * CHUNK)
            down = _dot(gated, wo_ref[ks, :])
            if first and part == 0:
                o_ref[...] = down
            else:
                o_ref[...] += down

    _first_rest(j, n_cols, step,
                       prologue=lambda: _rms_rows(xn_ref, h_ref, gpre_ref),
                       epilogue=lambda: _residual_rms_rows(o_ref, h_ref, gpost_ref))


def _gmlp_layer(h, g_pre, w_in, v_g, w_s, b_s, w_out, g_post, seq_len):
    t, d = h.shape
    d_a = w_out.shape[0]
    tm, tn = _tiles(seq_len).rows, _tiles(seq_len).cols
    nj = d_a // tn
    gpt = tn // CHUNK
    row = lambda m, j: (m, 0)
    vec_spec = pl.BlockSpec((1, d), lambda m, j: (0, 0))
    return pl.pallas_call(
        functools.partial(_gmlp_kernel, n_cols=nj),
        grid=(t // tm, nj),
        in_specs=[
            pl.BlockSpec((tm, d), row),
            vec_spec,
            pl.BlockSpec((d, tn), lambda m, j: (0, j)),
            pl.BlockSpec((d, tn), lambda m, j: (0, j + nj)),
            pl.BlockSpec((1, tn), lambda m, j: (0, j)),
            pl.BlockSpec((gpt, CHUNK, CHUNK), lambda m, j: (j, 0, 0)),
            pl.BlockSpec((gpt, CHUNK, 1), lambda m, j: (j, 0, 0)),
            pl.BlockSpec((tn, d), lambda m, j: (j, 0)),
            vec_spec,
        ],
        out_specs=pl.BlockSpec((tm, d), row),
        out_shape=jax.ShapeDtypeStruct((t, d), F32),
        scratch_shapes=[pltpu.VMEM((tm, d), BF16)],
        compiler_params=_params(2),
        name="gmlp_mixer",
    )(h, g_pre.reshape(1, d), w_in, w_in, v_g.reshape(1, d_a), w_s,
      b_s.reshape(b_s.shape[0], CHUNK, 1), w_out, g_post.reshape(1, d))


def _ffn_kernel(h_ref, gpre_ref, wg_ref, wv_ref, cwg_ref, cwv_ref, cbg_ref, cbv_ref,
                wd_ref, gpost_ref, o_ref, xn_ref, carry_ref, ext_ref, *, tiles_per_seq, n_cols):
    m = pl.program_id(0)
    f = pl.program_id(1)

    @pl.when(m % tiles_per_seq == 0)
    def _():
        carry_ref[f] = jnp.zeros(carry_ref.shape[1:], F32)

    def step(first):
        xn = xn_ref[...]
        tm = xn.shape[0]
        pad = V7X_SUBLANES
        ext_ref[:, 0:pad, :] = carry_ref[f]
        ext_ref[0, pad:, :] = _dot(xn, wg_ref[...])
        ext_ref[1, pad:, :] = _dot(xn, wv_ref[...])
        carry_ref[f] = ext_ref[:, tm:, :]

        def conv(which, cs, cw_ref, cb_ref):
            cw = cw_ref[:, cs]
            taps = [ext_ref[which, pad - s:pad - s + tm, cs] for s in (2, 1, 0)]
            return cb_ref[:, cs] + cw[0:1] * taps[0] + cw[1:2] * taps[1] + cw[2:3] * taps[2]

        width = wd_ref.shape[0] // DOWN_PROJ_SPLITS
        for part in range(DOWN_PROJ_SPLITS):
            cs = slice(part * width, (part + 1) * width)
            cg = conv(0, cs, cwg_ref, cbg_ref)
            cv = conv(1, cs, cwv_ref, cbv_ref)
            act = (cg * jax.nn.sigmoid(cg) * cv).astype(BF16)
            down = _dot(act, wd_ref[cs, :])
            if first and part == 0:
                o_ref[...] = down
            else:
                o_ref[...] += down

    _first_rest(f, n_cols, step,
                       prologue=lambda: _rms_rows(xn_ref, h_ref, gpre_ref),
                       epilogue=lambda: _residual_rms_rows(o_ref, h_ref, gpost_ref))


def _ffn_layer(h, layer, g_pre, w_up, conv_w, conv_b, w_down, g_post, seq_len):
    t, d = h.shape
    depth, d_ff, _ = w_down.shape
    tm, tf = _tiles(seq_len).rows, _tiles(seq_len).cols
    nf = d_ff // tf
    assert d_ff % tf == 0 and conv_w.shape[1] == 3
    taps = conv_w.shape[1]
    row = lambda m, f: (m, 0)
    vec_spec = pl.BlockSpec((None, 1, d), lambda m, f: (layer, 0, 0))
    conv_b3 = conv_b.reshape(depth, 1, 2 * d_ff)
    return pl.pallas_call(
        functools.partial(_ffn_kernel, tiles_per_seq=seq_len // tm, n_cols=nf),
        grid=(t // tm, nf),
        in_specs=[
            pl.BlockSpec((tm, d), row),
            vec_spec,
            pl.BlockSpec((None, d, tf), lambda m, f: (layer, 0, f)),
            pl.BlockSpec((None, d, tf), lambda m, f: (layer, 0, f + nf)),
            pl.BlockSpec((None, taps, tf), lambda m, f: (layer, 0, f)),
            pl.BlockSpec((None, taps, tf), lambda m, f: (layer, 0, f + nf)),
            pl.BlockSpec((None, 1, tf), lambda m, f: (layer, 0, f)),
            pl.BlockSpec((None, 1, tf), lambda m, f: (layer, 0, f + nf)),
            pl.BlockSpec((None, tf, d), lambda m, f: (layer, f, 0)),
            vec_spec,
        ],
        out_specs=pl.BlockSpec((tm, d), row),
        out_shape=jax.ShapeDtypeStruct((t, d), F32),
        scratch_shapes=[
            pltpu.VMEM((tm, d), BF16),
            pltpu.VMEM((nf, 2, V7X_SUBLANES, tf), F32),
            pltpu.VMEM((2, tm + V7X_SUBLANES, tf), F32),
        ],
        compiler_params=_params(2),
        name="conv_ffn",
    )(h, g_pre.reshape(depth, 1, d), w_up, w_up, conv_w, conv_w, conv_b3, conv_b3, w_down,
      g_post.reshape(depth, 1, d))


def _qkv_kernel(h_ref, gq_ref, gkv_ref, wq_ref, wk_ref, wv_ref, q_ref, k_ref, vt_ref,
                xq_ref, xkv_ref):
    j = pl.program_id(1)

    def normalize():
        gq, gkv = gq_ref[...], gkv_ref[...]
        for r0 in range(0, h_ref.shape[0], PRENORM_ROW_BLOCK):
            rows = slice(r0, r0 + PRENORM_ROW_BLOCK)
            x = h_ref[rows, :]
            xs = x * lax.rsqrt(jnp.mean(x * x, axis=-1, keepdims=True) + EPS)
            xq_ref[rows, :] = (xs * gq).astype(BF16)
            xkv_ref[rows, :] = (xs * gkv).astype(BF16)

    def project():
        scale = LOG2_E * HEAD_DIM ** -0.5
        q = _dot(xq_ref[...], wq_ref[...]) * scale
        k = _dot(xkv_ref[...], wk_ref[...])
        v = _dot(xkv_ref[...], wv_ref[...])
        for hh in range(q.shape[1] // HEAD_DIM):
            hs = slice(hh * HEAD_DIM, (hh + 1) * HEAD_DIM)
            q_ref[hh] = q[:, hs].astype(BF16)
            k_ref[hh] = k[:, hs].astype(BF16)
            vt_ref[hh] = v[:, hs].T.astype(BF16)

    def first_step():
        normalize()
        project()

    pl.when(j == 0)(first_step)
    pl.when(j > 0)(project)


def _qkv(h, g_q, g_kv, w_q, w_k, w_v, seq_len):
    t, d = h.shape
    d_b = w_q.shape[1]
    n_heads = d_b // HEAD_DIM
    tm, tn = _tiles(seq_len).rows, _tiles(seq_len).cols
    hpt = tn // HEAD_DIM
    w_spec = pl.BlockSpec((d, tn), lambda m, j: (0, j))
    vec_spec = pl.BlockSpec((1, d), lambda m, j: (0, 0))
    hd_spec = pl.BlockSpec((hpt, tm, HEAD_DIM), lambda m, j: (j, m, 0))
    return pl.pallas_call(
        _qkv_kernel,
        grid=(t // tm, d_b // tn),
        in_specs=[pl.BlockSpec((tm, d), lambda m, j: (m, 0)), vec_spec, vec_spec,
                  w_spec, w_spec, w_spec],
        out_specs=[hd_spec, hd_spec,
                   pl.BlockSpec((hpt, HEAD_DIM, tm), lambda m, j: (j, 0, m))],
        out_shape=[jax.ShapeDtypeStruct((n_heads, t, HEAD_DIM), BF16),
                   jax.ShapeDtypeStruct((n_heads, t, HEAD_DIM), BF16),
                   jax.ShapeDtypeStruct((n_heads, HEAD_DIM, t), BF16)],
        scratch_shapes=[pltpu.VMEM((tm, d), BF16), pltpu.VMEM((tm, d), BF16)],
        compiler_params=_params(2),
        name="qkv_proj",
    )(h, g_q.reshape(1, d), g_kv.reshape(1, d), w_q, w_k, w_v)


def _attn_kernel(q_ref, k_ref, vt_ref, later_ref, o_ref, acc_ref, *, tq, hp, qpb):
    key = lax.broadcasted_iota(jnp.int32, (tq, tq), 0)
    qry = lax.broadcasted_iota(jnp.int32, (tq, tq), 1)
    before_query = key < qry

    for sub in range(qpb):
        _attn_query_block(q_ref, k_ref, vt_ref, later_ref, o_ref, acc_ref, before_query,
                          pl.program_id(2) * qpb + sub, sub, tq=tq, hp=hp)


def _attn_query_block(q_ref, k_ref, vt_ref, later_ref, o_ref, acc_ref, before_query, i, sub,
                      *, tq, hp):
    qrows = slice(sub * tq, (sub + 1) * tq)

    def scores(hh, jb, diagonal):
        start = pl.multiple_of(jb * tq, tq)
        ks = k_ref[hh, pl.ds(start, tq), :]
        z = lax.dot_general(ks, q_ref[hh, qrows, :], (((1,), (1,)), ((), ())),
                            preferred_element_type=F32)
        m = jnp.maximum(z, 0.0)
        zm = z - m
        l = jnp.log(1.0 + jnp.exp2(zm - m)) * LOG2_E
        sp = m + l
        if diagonal:
            sp = jnp.where(before_query, sp, 0.0)
        sp_hi = sp.astype(BF16)
        sp_lo = (sp - sp_hi.astype(F32)).astype(BF16)
        return jnp.concatenate([sp_hi, sp_lo], axis=0), z

    def weights(split, z, carry, diagonal):
        suffix = _dot(later_ref[...], split)
        a = jnp.exp2(z + carry - suffix)
        if diagonal:
            a = jnp.where(before_query, a, 0.0)
        return a.astype(BF16), carry - suffix[0:1]

    def values(hh, jb, a, first=False, keep=None):
        start = pl.multiple_of(jb * tq, tq)
        vt = vt_ref[hh, :, pl.ds(start, tq)]
        if keep is not None:
            vt = jnp.where(keep, vt, jnp.zeros_like(vt))
        pv = _dot(vt, a)
        if first:
            acc_ref[sub, hh] = pv
        else:
            acc_ref[sub, hh] += pv

    heads = range(hp)
    prev = jnp.maximum(i - 1, 0)
    sd = [scores(hh, i, True) for hh in heads]
    sv = [scores(hh, prev, False) for hh in heads]
    wd = [weights(*sd[hh], jnp.zeros((1, tq), F32), True) for hh in heads]
    wv = [weights(*sv[hh], wd[hh][1], False) for hh in heads]
    for hh in heads:
        values(hh, i, wd[hh][0], first=True)
    for hh in heads:
        values(hh, prev, wv[hh][0], keep=i > 0)
    carries = tuple(wv[hh][1] for hh in heads)

    least_used = functools.reduce(jnp.maximum, [jnp.max(c) for c in carries])
    remaining = jnp.where(least_used > -ATTN_LOG2_UNDERFLOW, i - 1, 0)

    def body(n, cs):
        jb = i - 2 - n
        s = [scores(hh, jb, False) for hh in heads]
        w = [weights(*s[hh], cs[hh], False) for hh in heads]
        for hh in heads:
            values(hh, jb, w[hh][0])
        return tuple(w[hh][1] for hh in heads)

    lax.fori_loop(0, remaining, body, carries)
    o_ref[qrows, :] = jnp.concatenate([acc_ref[sub, hh].T for hh in heads],
                                      axis=1).astype(o_ref.dtype)


def _attention(q, k, vt, batch, seq_len):
    n_heads, t, dh = q.shape
    tq = _tiles(seq_len).attn
    hp = min(ATTN_HEADS_PER_STEP, n_heads)
    assert n_heads % hp == 0
    nq = seq_len // tq
    qpb = ATTN_QUERY_BLOCKS_PER_STEP
    assert nq % qpb == 0
    tri = jnp.triu(jnp.ones((tq, tq), BF16))
    later = jnp.concatenate([tri, tri], axis=1)
    return pl.pallas_call(
        functools.partial(_attn_kernel, tq=tq, hp=hp, qpb=qpb),
        grid=(batch, n_heads // hp, nq // qpb),
        in_specs=[
            pl.BlockSpec((hp, qpb * tq, dh), lambda b, g, i: (g, b * (nq // qpb) + i, 0)),
            pl.BlockSpec((hp, seq_len, dh), lambda b, g, i: (g, b, 0)),
            pl.BlockSpec((hp, dh, seq_len), lambda b, g, i: (g, 0, b)),
            pl.BlockSpec((tq, 2 * tq), lambda b, g, i: (0, 0)),
        ],
        out_specs=pl.BlockSpec((qpb * tq, hp * dh), lambda b, g, i: (b * (nq // qpb) + i, g)),
        out_shape=jax.ShapeDtypeStruct((t, n_heads * dh), BF16),
        scratch_shapes=[pltpu.VMEM((qpb, hp, dh, tq), F32)],
        compiler_params=_params(3),
        name="stickbreak_attn",
    )(q, k, vt, later)


def _proj_kernel(a_ref, w_ref, h_ref, g_ref, o_ref):
    o_ref[...] = h_ref[...] + _rms(_dot(a_ref[...], w_ref[...]), g_ref[...])


def _proj_residual(a, w, h, g, seq_len):
    t, d = h.shape
    d_in = a.shape[1]
    tm = _tiles(seq_len).proj_rows
    return pl.pallas_call(
        _proj_kernel,
        grid=(t // tm,),
        in_specs=[pl.BlockSpec((tm, d_in), lambda m: (m, 0)),
                  pl.BlockSpec((d_in, d), lambda m: (0, 0)),
                  pl.BlockSpec((tm, d), lambda m: (m, 0)),
                  pl.BlockSpec((1, d), lambda m: (0, 0))],
        out_specs=pl.BlockSpec((tm, d), lambda m: (m, 0)),
        out_shape=jax.ShapeDtypeStruct((t, d), F32),
        compiler_params=_params(1),
        name="attn_out_proj",
    )(a, w, h, g.reshape(1, d))


def kernel(x, pre_mix_g, post_mix_g, pre_ffn_g, post_ffn_g, a_w_in, a_v_norm_g, a_w_spatial, a_b_spatial, a_w_out, kv_norm_g, w_k, w_v, b_w_q, b_w_o, ffn_w_up, ffn_conv_w, ffn_conv_b, ffn_w_down):
    batch, seq_len, d = x.shape
    n_a = a_w_in.shape[0]
    n_b = b_w_q.shape[0]
    assert n_a == 1 and n_b == 1 and pre_mix_g.shape[0] == n_a + n_b
    assert a_w_spatial.shape[2] == CHUNK and w_k.shape[1] % HEAD_DIM == 0
    h = x.reshape(batch * seq_len, d)

    h = _gmlp_layer(h, pre_mix_g[0], a_w_in[0].astype(BF16), a_v_norm_g[0], a_w_spatial[0],
                    a_b_spatial[0], a_w_out[0].astype(BF16), post_mix_g[0], seq_len)
    w_up, w_down = ffn_w_up.astype(BF16), ffn_w_down.astype(BF16)
    ffn = functools.partial(_ffn_layer, g_pre=pre_ffn_g, w_up=w_up, conv_w=ffn_conv_w,
                            conv_b=ffn_conv_b, w_down=w_down, g_post=post_ffn_g, seq_len=seq_len)
    h = ffn(h, 0)

    q, k, vt = _qkv(h, pre_mix_g[1], kv_norm_g, b_w_q[0].astype(BF16), w_k.astype(BF16),
                    w_v.astype(BF16), seq_len)
    att = _attention(q, k, vt, batch, seq_len)
    h = _proj_residual(att, b_w_o[0].astype(BF16), h, post_mix_g[1], seq_len)
    h = ffn(h, 1)
    return h.reshape(batch, seq_len, d)
```

```python
import functools
from typing import NamedTuple

import jax
import jax.numpy as jnp
from jax import lax
from jax.experimental import pallas as pl
from jax.experimental.pallas import tpu as pltpu

CHUNK = 128
HEAD_DIM = 128
EPS = 1e-6
LOG2_E = 1.4426950408889634
ATTN_HEADS_PER_STEP = 8
ATTN_QUERY_BLOCKS_PER_STEP = 4
DOWN_PROJ_SPLITS = 2
ATTN_LOG2_UNDERFLOW = 160.0

V7X_LANES = 128
V7X_SUBLANES = 8
V7X_VMEM_LIMIT_BYTES = 60000 * 1024

NORM_ROW_BLOCK = V7X_SUBLANES
PRENORM_ROW_BLOCK = 2 * V7X_SUBLANES

F32 = jnp.float32
BF16 = jnp.bfloat16


class _Tiles(NamedTuple):
    rows: int
    proj_rows: int
    cols: int
    attn: int


def _tiles(seq_len):
    tiles = _Tiles(rows=1024, proj_rows=512, cols=512, attn=256)
    assert all(seq_len % n == 0 for n in (tiles.rows, tiles.proj_rows, tiles.attn))
    assert tiles.rows % CHUNK == 0
    return tiles


def _params(n_axes):
    return pltpu.CompilerParams(
        dimension_semantics=("arbitrary",) * n_axes,
        vmem_limit_bytes=V7X_VMEM_LIMIT_BYTES,
    )


def _rms(x, g):
    ms = jnp.mean(x * x, axis=-1, keepdims=True)
    return x * lax.rsqrt(ms + EPS) * g


def _residual_rms_rows(o_ref, h_ref, g_ref):
    g = g_ref[...]
    for r0 in range(0, o_ref.shape[0], NORM_ROW_BLOCK):
        rows = slice(r0, r0 + NORM_ROW_BLOCK)
        o_ref[rows, :] = h_ref[rows, :] + _rms(o_ref[rows, :], g)


def _rms_rows(dst_ref, x_ref, g_ref):
    g = g_ref[...]
    for r0 in range(0, x_ref.shape[0], PRENORM_ROW_BLOCK):
        rows = slice(r0, r0 + PRENORM_ROW_BLOCK)
        dst_ref[rows, :] = _rms(x_ref[rows, :], g).astype(dst_ref.dtype)


def _first_rest(i, n, step, prologue, epilogue):
    def first():
        prologue()
        step(True)

    pl.when(i == 0)(first)
    if n > 1:
        pl.when(i > 0)(functools.partial(step, False))
    pl.when(i == n - 1)(epilogue)


def _dot(a, b):
    return jnp.dot(a, b, preferred_element_type=F32)


def _gmlp_kernel(h_ref, gpre_ref, wu_ref, wv_ref, vg_ref, ws_ref, bs_ref, wo_ref,
                 gpost_ref, o_ref, xn_ref, *, n_cols):
    j = pl.program_id(1)

    def step(first):
        xn = xn_ref[...]
        u = jax.nn.gelu(_dot(xn, wu_ref[...]))
        v = jax.nn.gelu(_dot(xn, wv_ref[...]))
        tm, tn = u.shape
        row = lax.broadcasted_iota(jnp.int32, (CHUNK, CHUNK), 0)
        col = lax.broadcasted_iota(jnp.int32, (CHUNK, CHUNK), 1)
        causal = row >= col
        per_part = tn // CHUNK // DOWN_PROJ_SPLITS
        for part in range(DOWN_PROJ_SPLITS):
            cols = []
            for g in range(part * per_part, (part + 1) * per_part):
                gs = slice(g * CHUNK, (g + 1) * CHUNK)
                vn = _rms(v[:, gs], vg_ref[:, gs]).astype(BF16)
                w = jnp.where(causal, ws_ref[g], 0.0).astype(BF16)
                b = bs_ref[g]
                rows = []
                for c in range(tm // CHUNK):
                    cs = slice(c * CHUNK, (c + 1) * CHUNK)
                    mixed = _dot(w, vn[cs, :]) + b
                    rows.append(u[cs, gs] * mixed)
                cols.append(jnp.concatenate(rows, axis=0))
            gated = jnp.concatenate(cols, axis=1).astype(BF16)
            ks = slice(part * per_part * CHUNK, (part + 1) * per_part * CHUNK)
            down = _dot(gated, wo_ref[ks, :])
            if first and part == 0:
                o_ref[...] = down
            else:
                o_ref[...] += down

    _first_rest(j, n_cols, step,
                       prologue=lambda: _rms_rows(xn_ref, h_ref, gpre_ref),
                       epilogue=lambda: _residual_rms_rows(o_ref, h_ref, gpost_ref))


def _gmlp_layer(h, g_pre, w_in, v_g, w_s, b_s, w_out, g_post, seq_len):
    t, d = h.shape
    d_a = w_out.shape[0]
    tm, tn = _tiles(seq_len).rows, _tiles(seq_len).cols
    nj = d_a // tn
    gpt = tn // CHUNK
    row = lambda m, j: (m, 0)
    vec_spec = pl.BlockSpec((1, d), lambda m, j: (0, 0))
    return pl.pallas_call(
        functools.partial(_gmlp_kernel, n_cols=nj),
        grid=(t // tm, nj),
        in_specs=[
            pl.BlockSpec((tm, d), row),
            vec_spec,
            pl.BlockSpec((d, tn), lambda m, j: (0, j)),
            pl.BlockSpec((d, tn), lambda m, j: (0, j + nj)),
            pl.BlockSpec((1, tn), lambda m, j: (0, j)),
            pl.BlockSpec((gpt, CHUNK, CHUNK), lambda m, j: (j, 0, 0)),
            pl.BlockSpec((gpt, CHUNK, 1), lambda m, j: (j, 0, 0)),
            pl.BlockSpec((tn, d), lambda m, j: (j, 0)),
            vec_spec,
        ],
        out_specs=pl.BlockSpec((tm, d), row),
        out_shape=jax.ShapeDtypeStruct((t, d), F32),
        scratch_shapes=[pltpu.VMEM((tm, d), BF16)],
        compiler_params=_params(2),
        name="gmlp_mixer",
    )(h, g_pre.reshape(1, d), w_in, w_in, v_g.reshape(1, d_a), w_s,
      b_s.reshape(b_s.shape[0], CHUNK, 1), w_out, g_post.reshape(1, d))


def _ffn_kernel(h_ref, gpre_ref, wg_ref, wv_ref, cwg_ref, cwv_ref, cbg_ref, cbv_ref,
                wd_ref, gpost_ref, o_ref, xn_ref, carry_ref, ext_ref, *, tiles_per_seq, n_cols):
    m = pl.program_id(0)
    f = pl.program_id(1)

    @pl.when(m % tiles_per_seq == 0)
    def _():
        carry_ref[f] = jnp.zeros(carry_ref.shape[1:], F32)

    def step(first):
        xn = xn_ref[...]
        tm = xn.shape[0]
        pad = V7X_SUBLANES
        ext_ref[:, 0:pad, :] = carry_ref[f]
        ext_ref[0, pad:, :] = _dot(xn, wg_ref[...])
        ext_ref[1, pad:, :] = _dot(xn, wv_ref[...])
        carry_ref[f] = ext_ref[:, tm:, :]

        def conv(which, cs, cw_ref, cb_ref):
            cw = cw_ref[:, cs]
            taps = [ext_ref[which, pad - s:pad - s + tm, cs] for s in (2, 1, 0)]
            return cb_ref[:, cs] + cw[0:1] * taps[0] + cw[1:2] * taps[1] + cw[2:3] * taps[2]

        width = wd_ref.shape[0] // DOWN_PROJ_SPLITS
        for part in range(DOWN_PROJ_SPLITS):
            cs = slice(part * width, (part + 1) * width)
            cg = conv(0, cs, cwg_ref, cbg_ref)
            cv = conv(1, cs, cwv_ref, cbv_ref)
            act = (cg * jax.nn.sigmoid(cg) * cv).astype(BF16)
            down = _dot(act, wd_ref[cs, :])
            if first and part == 0:
                o_ref[...] = down
            else:
                o_ref[...] += down

    _first_rest(f, n_cols, step,
                       prologue=lambda: _rms_rows(xn_ref, h_ref, gpre_ref),
                       epilogue=lambda: _residual_rms_rows(o_ref, h_ref, gpost_ref))


def _ffn_layer(h, layer, g_pre, w_up, conv_w, conv_b, w_down, g_post, seq_len):
    t, d = h.shape
    depth, d_ff, _ = w_down.shape
    tm, tf = _tiles(seq_len).rows, _tiles(seq_len).cols
    nf = d_ff // tf
    assert d_ff % tf == 0 and conv_w.shape[1] == 3
    taps = conv_w.shape[1]
    row = lambda m, f: (m, 0)
    vec_spec = pl.BlockSpec((None, 1, d), lambda m, f: (layer, 0, 0))
    conv_b3 = conv_b.reshape(depth, 1, 2 * d_ff)
    return pl.pallas_call(
        functools.partial(_ffn_kernel, tiles_per_seq=seq_len // tm, n_cols=nf),
        grid=(t // tm, nf),
        in_specs=[
            pl.BlockSpec((tm, d), row),
            vec_spec,
            pl.BlockSpec((None, d, tf), lambda m, f: (layer, 0, f)),
            pl.BlockSpec((None, d, tf), lambda m, f: (layer, 0, f + nf)),
            pl.BlockSpec((None, taps, tf), lambda m, f: (layer, 0, f)),
            pl.BlockSpec((None, taps, tf), lambda m, f: (layer, 0, f + nf)),
            pl.BlockSpec((None, 1, tf), lambda m, f: (layer, 0, f)),
            pl.BlockSpec((None, 1, tf), lambda m, f: (layer, 0, f + nf)),
            pl.BlockSpec((None, tf, d), lambda m, f: (layer, f, 0)),
            vec_spec,
        ],
        out_specs=pl.BlockSpec((tm, d), row),
        out_shape=jax.ShapeDtypeStruct((t, d), F32),
        scratch_shapes=[
            pltpu.VMEM((tm, d), BF16),
            pltpu.VMEM((nf, 2, V7X_SUBLANES, tf), F32),
            pltpu.VMEM((2, tm + V7X_SUBLANES, tf), F32),
        ],
        compiler_params=_params(2),
        name="conv_ffn",
    )(h, g_pre.reshape(depth, 1, d), w_up, w_up, conv_w, conv_w, conv_b3, conv_b3, w_down,
      g_post.reshape(depth, 1, d))


def _qkv_kernel(h_ref, gq_ref, gkv_ref, wq_ref, wk_ref, wv_ref, q_ref, k_ref, vt_ref,
                xq_ref, xkv_ref):
    j = pl.program_id(1)

    def normalize():
        gq, gkv = gq_ref[...], gkv_ref[...]
        for r0 in range(0, h_ref.shape[0], PRENORM_ROW_BLOCK):
            rows = slice(r0, r0 + PRENORM_ROW_BLOCK)
            x = h_ref[rows, :]
            xs = x * lax.rsqrt(jnp.mean(x * x, axis=-1, keepdims=True) + EPS)
            xq_ref[rows, :] = (xs * gq).astype(BF16)
            xkv_ref[rows, :] = (xs * gkv).astype(BF16)

    def project():
        scale = LOG2_E * HEAD_DIM ** -0.5
        q = _dot(xq_ref[...], wq_ref[...]) * scale
        k = _dot(xkv_ref[...], wk_ref[...])
        v = _dot(xkv_ref[...], wv_ref[...])
        for hh in range(q.shape[1] // HEAD_DIM):
            hs = slice(hh * HEAD_DIM, (hh + 1) * HEAD_DIM)
            q_ref[hh] = q[:, hs].astype(BF16)
            k_ref[hh] = k[:, hs].astype(BF16)
            vt_ref[hh] = v[:, hs].T.astype(BF16)

    def first_step():
        normalize()
        project()

    pl.when(j == 0)(first_step)
    pl.when(j > 0)(project)


def _qkv(h, g_q, g_kv, w_q, w_k, w_v, seq_len):
    t, d = h.shape
    d_b = w_q.shape[1]
    n_heads = d_b // HEAD_DIM
    tm, tn = _tiles(seq_len).rows, _tiles(seq_len).cols
    hpt = tn // HEAD_DIM
    w_spec = pl.BlockSpec((d, tn), lambda m, j: (0, j))
    vec_spec = pl.BlockSpec((1, d), lambda m, j: (0, 0))
    hd_spec = pl.BlockSpec((hpt, tm, HEAD_DIM), lambda m, j: (j, m, 0))
    return pl.pallas_call(
        _qkv_kernel,
        grid=(t // tm, d_b // tn),
        in_specs=[pl.BlockSpec((tm, d), lambda m, j: (m, 0)), vec_spec, vec_spec,
                  w_spec, w_spec, w_spec],
        out_specs=[hd_spec, hd_spec,
                   pl.BlockSpec((hpt, HEAD_DIM, tm), lambda m, j: (j, 0, m))],
        out_shape=[jax.ShapeDtypeStruct((n_heads, t, HEAD_DIM), BF16),
                   jax.ShapeDtypeStruct((n_heads, t, HEAD_DIM), BF16),
                   jax.ShapeDtypeStruct((n_heads, HEAD_DIM, t), BF16)],
        scratch_shapes=[pltpu.VMEM((tm, d), BF16), pltpu.VMEM((tm, d), BF16)],
        compiler_params=_params(2),
        name="qkv_proj",
    )(h, g_q.reshape(1, d), g_kv.reshape(1, d), w_q, w_k, w_v)


def _attn_kernel(q_ref, k_ref, vt_ref, later_ref, o_ref, acc_ref, *, tq, hp, qpb):
    key = lax.broadcasted_iota(jnp.int32, (tq, tq), 0)
    qry = lax.broadcasted_iota(jnp.int32, (tq, tq), 1)
    before_query = key < qry

    for sub in range(qpb):
        _attn_query_block(q_ref, k_ref, vt_ref, later_ref, o_ref, acc_ref, before_query,
                          pl.program_id(2) * qpb + sub, sub, tq=tq, hp=hp)


def _attn_query_block(q_ref, k_ref, vt_ref, later_ref, o_ref, acc_ref, before_query, i, sub,
                      *, tq, hp):
    qrows = slice(sub * tq, (sub + 1) * tq)

    def scores(hh, jb, diagonal):
        start = pl.multiple_of(jb * tq, tq)
        ks = k_ref[hh, pl.ds(start, tq), :]
        z = lax.dot_general(ks, q_ref[hh, qrows, :], (((1,), (1,)), ((), ())),
                            preferred_element_type=F32)
        m = jnp.maximum(z, 0.0)
        zm = z - m
        l = jnp.log(1.0 + jnp.exp2(zm - m)) * LOG2_E
        sp = m + l
        if diagonal:
            sp = jnp.where(before_query, sp, 0.0)
        sp_hi = sp.astype(BF16)
        sp_lo = (sp - sp_hi.astype(F32)).astype(BF16)
        return jnp.concatenate([sp_hi, sp_lo], axis=0), z

    def weights(split, z, carry, diagonal):
        suffix = _dot(later_ref[...], split)
        a = jnp.exp2(z + carry - suffix)
        if diagonal:
            a = jnp.where(before_query, a, 0.0)
        return a.astype(BF16), carry - suffix[0:1]

    def values(hh, jb, a, first=False, keep=None):
        start = pl.multiple_of(jb * tq, tq)
        vt = vt_ref[hh, :, pl.ds(start, tq)]
        if keep is not None:
            vt = jnp.where(keep, vt, jnp.zeros_like(vt))
        pv = _dot(vt, a)
        if first:
            acc_ref[sub, hh] = pv
        else:
            acc_ref[sub, hh] += pv

    heads = range(hp)
    prev = jnp.maximum(i - 1, 0)
    sd = [scores(hh, i, True) for hh in heads]
    sv = [scores(hh, prev, False) for hh in heads]
    wd = [weights(*sd[hh], jnp.zeros((1, tq), F32), True) for hh in heads]
    wv = [weights(*sv[hh], wd[hh][1], False) for hh in heads]
    for hh in heads:
        values(hh, i, wd[hh][0], first=True)
    for hh in heads:
        values(hh, prev, wv[hh][0], keep=i > 0)
    carries = tuple(wv[hh][1] for hh in heads)

    least_used = functools.reduce(jnp.maximum, [jnp.max(c) for c in carries])
    remaining = jnp.where(least_used > -ATTN_LOG2_UNDERFLOW, i - 1, 0)

    def body(n, cs):
        jb = i - 2 - n
        s = [scores(hh, jb, False) for hh in heads]
        w = [weights(*s[hh], cs[hh], False) for hh in heads]
        for hh in heads:
            values(hh, jb, w[hh][0])
        return tuple(w[hh][1] for hh in heads)

    lax.fori_loop(0, remaining, body, carries)
    o_ref[qrows, :] = jnp.concatenate([acc_ref[sub, hh].T for hh in heads],
                                      axis=1).astype(o_ref.dtype)


def _attention(q, k, vt, batch, seq_len):
    n_heads, t, dh = q.shape
    tq = _tiles(seq_len).attn
    hp = min(ATTN_HEADS_PER_STEP, n_heads)
    assert n_heads % hp == 0
    nq = seq_len // tq
    qpb = ATTN_QUERY_BLOCKS_PER_STEP
    assert nq % qpb == 0
    tri = jnp.triu(jnp.ones((tq, tq), BF16))
    later = jnp.concatenate([tri, tri], axis=1)
    return pl.pallas_call(
        functools.partial(_attn_kernel, tq=tq, hp=hp, qpb=qpb),
        grid=(batch, n_heads // hp, nq // qpb),
        in_specs=[
            pl.BlockSpec((hp, qpb * tq, dh), lambda b, g, i: (g, b * (nq // qpb) + i, 0)),
            pl.BlockSpec((hp, seq_len, dh), lambda b, g, i: (g, b, 0)),
            pl.BlockSpec((hp, dh, seq_len), lambda b, g, i: (g, 0, b)),
            pl.BlockSpec((tq, 2 * tq), lambda b, g, i: (0, 0)),
        ],
        out_specs=pl.BlockSpec((qpb * tq, hp * dh), lambda b, g, i: (b * (nq // qpb) + i, g)),
        out_shape=jax.ShapeDtypeStruct((t, n_heads * dh), BF16),
        scratch_shapes=[pltpu.VMEM((qpb, hp, dh, tq), F32)],
        compiler_params=_params(3),
        name="stickbreak_attn",
    )(q, k, vt, later)


def _proj_kernel(a_ref, w_ref, h_ref, g_ref, o_ref):
    o_ref[...] = h_ref[...] + _rms(_dot(a_ref[...], w_ref[...]), g_ref[...])


def _proj_residual(a, w, h, g, seq_len):
    t, d = h.shape
    d_in = a.shape[1]
    tm = _tiles(seq_len).proj_rows
    return pl.pallas_call(
        _proj_kernel,
        grid=(t // tm,),
        in_specs=[pl.BlockSpec((tm, d_in), lambda m: (m, 0)),
                  pl.BlockSpec((d_in, d), lambda m: (0, 0)),
                  pl.BlockSpec((tm, d), lambda m: (m, 0)),
                  pl.BlockSpec((1, d), lambda m: (0, 0))],
        out_specs=pl.BlockSpec((tm, d), lambda m: (m, 0)),
        out_shape=jax.ShapeDtypeStruct((t, d), F32),
        compiler_params=_params(1),
        name="attn_out_proj",
    )(a, w, h, g.reshape(1, d))


def kernel(x, pre_mix_g, post_mix_g, pre_ffn_g, post_ffn_g, a_w_in, a_v_norm_g, a_w_spatial, a_b_spatial, a_w_out, kv_norm_g, w_k, w_v, b_w_q, b_w_o, ffn_w_up, ffn_conv_w, ffn_conv_b, ffn_w_down):
    batch, seq_len, d = x.shape
    n_a = a_w_in.shape[0]
    n_b = b_w_q.shape[0]
    assert n_a == 1 and n_b == 1 and pre_mix_g.shape[0] == n_a + n_b
    assert a_w_spatial.shape[2] == CHUNK and w_k.shape[1] % HEAD_DIM == 0
    h = x.reshape(batch * seq_len, d)

    h = _gmlp_layer(h, pre_mix_g[0], a_w_in[0].astype(BF16), a_v_norm_g[0], a_w_spatial[0],
                    a_b_spatial[0], a_w_out[0].astype(BF16), post_mix_g[0], seq_len)
    w_up, w_down = ffn_w_up.astype(BF16), ffn_w_down.astype(BF16)
    ffn = functools.partial(_ffn_layer, g_pre=pre_ffn_g, w_up=w_up, conv_w=ffn_conv_w,
                            conv_b=ffn_conv_b, w_down=w_down, g_post=post_ffn_g, seq_len=seq_len)
    h = ffn(h, 0)

    q, k, vt = _qkv(h, pre_mix_g[1], kv_norm_g, b_w_q[0].astype(BF16), w_k.astype(BF16),
                    w_v.astype(BF16), seq_len)
    att = _attention(q, k, vt, batch, seq_len)
    h = _proj_residual(att, b_w_o[0].astype(BF16), h, post_mix_g[1], seq_len)
    h = ffn(h, 1)
    return h.reshape(batch, seq_len, d)
```

```python
import functools
from typing import NamedTuple

import jax
import jax.numpy as jnp
from jax import lax
from jax.experimental import pallas as pl
from jax.experimental.pallas import tpu as pltpu

CHUNK = 128
HEAD_DIM = 128
EPS = 1e-6
LOG2_E = 1.4426950408889634
ATTN_HEADS_PER_STEP = 8
ATTN_QUERY_BLOCKS_PER_STEP = 2
DOWN_PROJ_SPLITS = 2
ATTN_LOG2_UNDERFLOW = 160.0

V7X_LANES = 128
V7X_SUBLANES = 8
V7X_VMEM_LIMIT_BYTES = 60000 * 1024

NORM_ROW_BLOCK = V7X_SUBLANES
PRENORM_ROW_BLOCK = 2 * V7X_SUBLANES

F32 = jnp.float32
BF16 = jnp.bfloat16


class _Tiles(NamedTuple):
    rows: int
    proj_rows: int
    cols: int
    attn: int


def _tiles(seq_len):
    tiles = _Tiles(rows=1024, proj_rows=512, cols=512, attn=256)
    assert all(seq_len % n == 0 for n in (tiles.rows, tiles.proj_rows, tiles.attn))
    assert tiles.rows % CHUNK == 0
    return tiles


def _params(n_axes):
    return pltpu.CompilerParams(
        dimension_semantics=("arbitrary",) * n_axes,
        vmem_limit_bytes=V7X_VMEM_LIMIT_BYTES,
    )


def _rms(x, g):
    ms = jnp.mean(x * x, axis=-1, keepdims=True)
    return x * lax.rsqrt(ms + EPS) * g


def _residual_rms_rows(o_ref, h_ref, g_ref):
    g = g_ref[...]
    for r0 in range(0, o_ref.shape[0], NORM_ROW_BLOCK):
        rows = slice(r0, r0 + NORM_ROW_BLOCK)
        o_ref[rows, :] = h_ref[rows, :] + _rms(o_ref[rows, :], g)


def _rms_rows(dst_ref, x_ref, g_ref):
    g = g_ref[...]
    for r0 in range(0, x_ref.shape[0], PRENORM_ROW_BLOCK):
        rows = slice(r0, r0 + PRENORM_ROW_BLOCK)
        dst_ref[rows, :] = _rms(x_ref[rows, :], g).astype(dst_ref.dtype)


def _first_rest(i, n, step, prologue, epilogue):
    def first():
        prologue()
        step(True)

    pl.when(i == 0)(first)
    if n > 1:
        pl.when(i > 0)(functools.partial(step, False))
    pl.when(i == n - 1)(epilogue)


def _dot(a, b):
    return jnp.dot(a, b, preferred_element_type=F32)


def _gmlp_kernel(h_ref, gpre_ref, wu_ref, wv_ref, vg_ref, ws_ref, bs_ref, wo_ref,
                 gpost_ref, o_ref, xn_ref, *, n_cols):
    j = pl.program_id(1)

    def step(first):
        xn = xn_ref[...]
        v = jax.nn.gelu(_dot(xn, wv_ref[...]))
        u = jax.nn.gelu(_dot(xn, wu_ref[...]))
        tm, tn = u.shape
        row = lax.broadcasted_iota(jnp.int32, (CHUNK, CHUNK), 0)
        col = lax.broadcasted_iota(jnp.int32, (CHUNK, CHUNK), 1)
        causal = row >= col
        per_part = tn // CHUNK // DOWN_PROJ_SPLITS
        for part in range(DOWN_PROJ_SPLITS):
            cols = []
            for g in range(part * per_part, (part + 1) * per_part):
                gs = slice(g * CHUNK, (g + 1) * CHUNK)
                vn = _rms(v[:, gs], vg_ref[:, gs]).astype(BF16)
                w = jnp.where(causal, ws_ref[g], 0.0).astype(BF16)
                b = bs_ref[g]
                rows = []
                for c in range(tm // CHUNK):
                    cs = slice(c * CHUNK, (c + 1) * CHUNK)
                    mixed = _dot(w, vn[cs, :]) + b
                    rows.append(u[cs, gs] * mixed)
                cols.append(jnp.concatenate(rows, axis=0))
            gated = jnp.concatenate(cols, axis=1).astype(BF16)
            ks = slice(part * per_part * CHUNK, (part + 1) * per_part * CHUNK)
            down = _dot(gated, wo_ref[ks, :])
            if first and part == 0:
                o_ref[...] = down
            else:
                o_ref[...] += down

    _first_rest(j, n_cols, step,
                       prologue=lambda: _rms_rows(xn_ref, h_ref, gpre_ref),
                       epilogue=lambda: _residual_rms_rows(o_ref, h_ref, gpost_ref))


def _gmlp_layer(h, g_pre, w_in, v_g, w_s, b_s, w_out, g_post, seq_len):
    t, d = h.shape
    d_a = w_out.shape[0]
    tm, tn = _tiles(seq_len).rows, _tiles(seq_len).cols
    nj = d_a // tn
    gpt = tn // CHUNK
    row = lambda m, j: (m, 0)
    vec_spec = pl.BlockSpec((1, d), lambda m, j: (0, 0))
    return pl.pallas_call(
        functools.partial(_gmlp_kernel, n_cols=nj),
        grid=(t // tm, nj),
        in_specs=[
            pl.BlockSpec((tm, d), row),
            vec_spec,
            pl.BlockSpec((d, tn), lambda m, j: (0, j)),
            pl.BlockSpec((d, tn), lambda m, j: (0, j + nj)),
            pl.BlockSpec((1, tn), lambda m, j: (0, j)),
            pl.BlockSpec((gpt, CHUNK, CHUNK), lambda m, j: (j, 0, 0)),
            pl.BlockSpec((gpt, CHUNK, 1), lambda m, j: (j, 0, 0)),
            pl.BlockSpec((tn, d), lambda m, j: (j, 0)),
            vec_spec,
        ],
        out_specs=pl.BlockSpec((tm, d), row),
        out_shape=jax.ShapeDtypeStruct((t, d), F32),
        scratch_shapes=[pltpu.VMEM((tm, d), BF16)],
        compiler_params=_params(2),
        name="gmlp_mixer",
    )(h, g_pre.reshape(1, d), w_in, w_in, v_g.reshape(1, d_a), w_s,
      b_s.reshape(b_s.shape[0], CHUNK, 1), w_out, g_post.reshape(1, d))


def _ffn_kernel(h_ref, gpre_ref, wg_ref, wv_ref, cwg_ref, cwv_ref, cbg_ref, cbv_ref,
                wd_ref, gpost_ref, o_ref, xn_ref, carry_ref, ext_ref, *, tiles_per_seq, n_cols):
    m = pl.program_id(0)
    f = pl.program_id(1)

    @pl.when(m % tiles_per_seq == 0)
    def _():
        carry_ref[f] = jnp.zeros(carry_ref.shape[1:], F32)

    def step(first):
        xn = xn_ref[...]
        tm = xn.shape[0]
        pad = V7X_SUBLANES
        ext_ref[:, 0:pad, :] = carry_ref[f]
        ext_ref[0, pad:, :] = _dot(xn, wg_ref[...])
        ext_ref[1, pad:, :] = _dot(xn, wv_ref[...])
        carry_ref[f] = ext_ref[:, tm:, :]

        def conv(which, cs, cw_ref, cb_ref):
            cw = cw_ref[:, cs]
            taps = [ext_ref[which, pad - s:pad - s + tm, cs] for s in (2, 1, 0)]
            return cb_ref[:, cs] + cw[0:1] * taps[0] + cw[1:2] * taps[1] + cw[2:3] * taps[2]

        width = wd_ref.shape[0] // DOWN_PROJ_SPLITS
        for part in range(DOWN_PROJ_SPLITS):
            cs = slice(part * width, (part + 1) * width)
            cg = conv(0, cs, cwg_ref, cbg_ref)
            cv = conv(1, cs, cwv_ref, cbv_ref)
            act = (cg * jax.nn.sigmoid(cg) * cv).astype(BF16)
            down = _dot(act, wd_ref[cs, :])
            if first and part == 0:
                o_ref[...] = down
            else:
                o_ref[...] += down

    _first_rest(f, n_cols, step,
                       prologue=lambda: _rms_rows(xn_ref, h_ref, gpre_ref),
                       epilogue=lambda: _residual_rms_rows(o_ref, h_ref, gpost_ref))


def _ffn_layer(h, layer, g_pre, w_up, conv_w, conv_b, w_down, g_post, seq_len):
    t, d = h.shape
    depth, d_ff, _ = w_down.shape
    tm, tf = _tiles(seq_len).rows, _tiles(seq_len).cols
    nf = d_ff // tf
    assert d_ff % tf == 0 and conv_w.shape[1] == 3
    taps = conv_w.shape[1]
    row = lambda m, f: (m, 0)
    vec_spec = pl.BlockSpec((None, 1, d), lambda m, f: (layer, 0, 0))
    conv_b3 = conv_b.reshape(depth, 1, 2 * d_ff)
    return pl.pallas_call(
        functools.partial(_ffn_kernel, tiles_per_seq=seq_len // tm, n_cols=nf),
        grid=(t // tm, nf),
        in_specs=[
            pl.BlockSpec((tm, d), row),
            vec_spec,
            pl.BlockSpec((None, d, tf), lambda m, f: (layer, 0, f)),
            pl.BlockSpec((None, d, tf), lambda m, f: (layer, 0, f + nf)),
            pl.BlockSpec((None, taps, tf), lambda m, f: (layer, 0, f)),
            pl.BlockSpec((None, taps, tf), lambda m, f: (layer, 0, f + nf)),
            pl.BlockSpec((None, 1, tf), lambda m, f: (layer, 0, f)),
            pl.BlockSpec((None, 1, tf), lambda m, f: (layer, 0, f + nf)),
            pl.BlockSpec((None, tf, d), lambda m, f: (layer, f, 0)),
            vec_spec,
        ],
        out_specs=pl.BlockSpec((tm, d), row),
        out_shape=jax.ShapeDtypeStruct((t, d), F32),
        scratch_shapes=[
            pltpu.VMEM((tm, d), BF16),
            pltpu.VMEM((nf, 2, V7X_SUBLANES, tf), F32),
            pltpu.VMEM((2, tm + V7X_SUBLANES, tf), F32),
        ],
        compiler_params=_params(2),
        name="conv_ffn",
    )(h, g_pre.reshape(depth, 1, d), w_up, w_up, conv_w, conv_w, conv_b3, conv_b3, w_down,
      g_post.reshape(depth, 1, d))


def _qkv_kernel(h_ref, gq_ref, gkv_ref, wq_ref, wk_ref, wv_ref, q_ref, k_ref, vt_ref,
                xq_ref, xkv_ref):
    j = pl.program_id(1)

    def normalize():
        gq, gkv = gq_ref[...], gkv_ref[...]
        for r0 in range(0, h_ref.shape[0], PRENORM_ROW_BLOCK):
            rows = slice(r0, r0 + PRENORM_ROW_BLOCK)
            x = h_ref[rows, :]
            xs = x * lax.rsqrt(jnp.mean(x * x, axis=-1, keepdims=True) + EPS)
            xq_ref[rows, :] = (xs * gq).astype(BF16)
            xkv_ref[rows, :] = (xs * gkv).astype(BF16)

    def project():
        scale = LOG2_E * HEAD_DIM ** -0.5
        v = _dot(xkv_ref[...], wv_ref[...])
        k = _dot(xkv_ref[...], wk_ref[...])
        q = _dot(xq_ref[...], wq_ref[...]) * scale
        for hh in range(q.shape[1] // HEAD_DIM):
            hs = slice(hh * HEAD_DIM, (hh + 1) * HEAD_DIM)
            q_ref[hh] = q[:, hs].astype(BF16)
            k_ref[hh] = k[:, hs].astype(BF16)
            vt_ref[hh] = v[:, hs].T.astype(BF16)

    def first_step():
        normalize()
        project()

    pl.when(j == 0)(first_step)
    pl.when(j > 0)(project)


def _qkv(h, g_q, g_kv, w_q, w_k, w_v, seq_len):
    t, d = h.shape
    d_b = w_q.shape[1]
    n_heads = d_b // HEAD_DIM
    tm, tn = _tiles(seq_len).rows, _tiles(seq_len).cols
    hpt = tn // HEAD_DIM
    w_spec = pl.BlockSpec((d, tn), lambda m, j: (0, j))
    vec_spec = pl.BlockSpec((1, d), lambda m, j: (0, 0))
    hd_spec = pl.BlockSpec((hpt, tm, HEAD_DIM), lambda m, j: (j, m, 0))
    return pl.pallas_call(
        _qkv_kernel,
        grid=(t // tm, d_b // tn),
        in_specs=[pl.BlockSpec((tm, d), lambda m, j: (m, 0)), vec_spec, vec_spec,
                  w_spec, w_spec, w_spec],
        out_specs=[hd_spec, hd_spec,
                   pl.BlockSpec((hpt, HEAD_DIM, tm), lambda m, j: (j, 0, m))],
        out_shape=[jax.ShapeDtypeStruct((n_heads, t, HEAD_DIM), BF16),
                   jax.ShapeDtypeStruct((n_heads, t, HEAD_DIM), BF16),
                   jax.ShapeDtypeStruct((n_heads, HEAD_DIM, t), BF16)],
        scratch_shapes=[pltpu.VMEM((tm, d), BF16), pltpu.VMEM((tm, d), BF16)],
        compiler_params=_params(2),
        name="qkv_proj",
    )(h, g_q.reshape(1, d), g_kv.reshape(1, d), w_q, w_k, w_v)


def _attn_kernel(q_ref, k_ref, vt_ref, later_ref, o_ref, acc_ref, *, tq, hp, qpb):
    key = lax.broadcasted_iota(jnp.int32, (tq, tq), 0)
    qry = lax.broadcasted_iota(jnp.int32, (tq, tq), 1)
    before_query = key < qry

    for sub in range(qpb):
        _attn_query_block(q_ref, k_ref, vt_ref, later_ref, o_ref, acc_ref, before_query,
                          pl.program_id(2) * qpb + sub, sub, tq=tq, hp=hp)


def _attn_query_block(q_ref, k_ref, vt_ref, later_ref, o_ref, acc_ref, before_query, i, sub,
                      *, tq, hp):
    qrows = slice(sub * tq, (sub + 1) * tq)

    def scores(hh, jb, diagonal):
        start = pl.multiple_of(jb * tq, tq)
        ks = k_ref[hh, pl.ds(start, tq), :]
        z = lax.dot_general(ks, q_ref[hh, qrows, :], (((1,), (1,)), ((), ())),
                            preferred_element_type=F32)
        m = jnp.maximum(z, 0.0)
        zm = z - m
        l = jnp.log(1.0 + jnp.exp2(zm - m)) * LOG2_E
        sp = m + l
        if diagonal:
            sp = jnp.where(before_query, sp, 0.0)
        sp_hi = sp.astype(BF16)
        sp_lo = (sp - sp_hi.astype(F32)).astype(BF16)
        return jnp.concatenate([sp_hi, sp_lo], axis=0), z

    def weights(split, z, carry, diagonal):
        suffix = _dot(later_ref[...], split)
        a = jnp.exp2(z + carry - suffix)
        if diagonal:
            a = jnp.where(before_query, a, 0.0)
        return a.astype(BF16), carry - suffix[0:1]

    def values(hh, jb, a, first=False, keep=None):
        start = pl.multiple_of(jb * tq, tq)
        vt = vt_ref[hh, :, pl.ds(start, tq)]
        if keep is not None:
            vt = jnp.where(keep, vt, jnp.zeros_like(vt))
        pv = _dot(vt, a)
        if first:
            acc_ref[sub, hh] = pv
        else:
            acc_ref[sub, hh] += pv

    heads = range(hp)
    prev = jnp.maximum(i - 1, 0)
    sd = [scores(hh, i, True) for hh in heads]
    sv = [scores(hh, prev, False) for hh in heads]
    wd = [weights(*sd[hh], jnp.zeros((1, tq), F32), True) for hh in heads]
    wv = [weights(*sv[hh], wd[hh][1], False) for hh in heads]
    for hh in heads:
        values(hh, i, wd[hh][0], first=True)
    for hh in heads:
        values(hh, prev, wv[hh][0], keep=i > 0)
    carries = tuple(wv[hh][1] for hh in heads)

    least_used = functools.reduce(jnp.maximum, [jnp.max(c) for c in carries])
    remaining = jnp.where(least_used > -ATTN_LOG2_UNDERFLOW, i - 1, 0)

    def body(n, cs):
        jb = i - 2 - n
        s = [scores(hh, jb, False) for hh in heads]
        w = [weights(*s[hh], cs[hh], False) for hh in heads]
        for hh in heads:
            values(hh, jb, w[hh][0])
        return tuple(w[hh][1] for hh in heads)

    lax.fori_loop(0, remaining, body, carries)
    o_ref[qrows, :] = jnp.concatenate([acc_ref[sub, hh].T for hh in heads],
                                      axis=1).astype(o_ref.dtype)


def _attention(q, k, vt, batch, seq_len):
    n_heads, t, dh = q.shape
    tq = _tiles(seq_len).attn
    hp = min(ATTN_HEADS_PER_STEP, n_heads)
    assert n_heads % hp == 0
    nq = seq_len // tq
    qpb = ATTN_QUERY_BLOCKS_PER_STEP
    assert nq % qpb == 0
    tri = jnp.triu(jnp.ones((tq, tq), BF16))
    later = jnp.concatenate([tri, tri], axis=1)
    return pl.pallas_call(
        functools.partial(_attn_kernel, tq=tq, hp=hp, qpb=qpb),
        grid=(batch, n_heads // hp, nq // qpb),
        in_specs=[
            pl.BlockSpec((hp, qpb * tq, dh), lambda b, g, i: (g, b * (nq // qpb) + i, 0)),
            pl.BlockSpec((hp, seq_len, dh), lambda b, g, i: (g, b, 0)),
            pl.BlockSpec((hp, dh, seq_len), lambda b, g, i: (g, 0, b)),
            pl.BlockSpec((tq, 2 * tq), lambda b, g, i: (0, 0)),
        ],
        out_specs=pl.BlockSpec((qpb * tq, hp * dh), lambda b, g, i: (b * (nq // qpb) + i, g)),
        out_shape=jax.ShapeDtypeStruct((t, n_heads * dh), BF16),
        scratch_shapes=[pltpu.VMEM((qpb, hp, dh, tq), F32)],
        compiler_params=_params(3),
        name="stickbreak_attn",
    )(q, k, vt, later)


def _proj_kernel(a_ref, w_ref, h_ref, g_ref, o_ref):
    o_ref[...] = h_ref[...] + _rms(_dot(a_ref[...], w_ref[...]), g_ref[...])


def _proj_residual(a, w, h, g, seq_len):
    t, d = h.shape
    d_in = a.shape[1]
    tm = _tiles(seq_len).proj_rows
    return pl.pallas_call(
        _proj_kernel,
        grid=(t // tm,),
        in_specs=[pl.BlockSpec((tm, d_in), lambda m: (m, 0)),
                  pl.BlockSpec((d_in, d), lambda m: (0, 0)),
                  pl.BlockSpec((tm, d), lambda m: (m, 0)),
                  pl.BlockSpec((1, d), lambda m: (0, 0))],
        out_specs=pl.BlockSpec((tm, d), lambda m: (m, 0)),
        out_shape=jax.ShapeDtypeStruct((t, d), F32),
        compiler_params=_params(1),
        name="attn_out_proj",
    )(a, w, h, g.reshape(1, d))


def kernel(x, pre_mix_g, post_mix_g, pre_ffn_g, post_ffn_g, a_w_in, a_v_norm_g, a_w_spatial, a_b_spatial, a_w_out, kv_norm_g, w_k, w_v, b_w_q, b_w_o, ffn_w_up, ffn_conv_w, ffn_conv_b, ffn_w_down):
    batch, seq_len, d = x.shape
    n_a = a_w_in.shape[0]
    n_b = b_w_q.shape[0]
    assert n_a == 1 and n_b == 1 and pre_mix_g.shape[0] == n_a + n_b
    assert a_w_spatial.shape[2] == CHUNK and w_k.shape[1] % HEAD_DIM == 0
    h = x.reshape(batch * seq_len, d)

    h = _gmlp_layer(h, pre_mix_g[0], a_w_in[0].astype(BF16), a_v_norm_g[0], a_w_spatial[0],
                    a_b_spatial[0], a_w_out[0].astype(BF16), post_mix_g[0], seq_len)
    w_up, w_down = ffn_w_up.astype(BF16), ffn_w_down.astype(BF16)
    ffn = functools.partial(_ffn_layer, g_pre=pre_ffn_g, w_up=w_up, conv_w=ffn_conv_w,
                            conv_b=ffn_conv_b, w_down=w_down, g_post=post_ffn_g, seq_len=seq_len)
    h = ffn(h, 0)

    q, k, vt = _qkv(h, pre_mix_g[1], kv_norm_g, b_w_q[0].astype(BF16), w_k.astype(BF16),
                    w_v.astype(BF16), seq_len)
    att = _attention(q, k, vt, batch, seq_len)
    h = _proj_residual(att, b_w_o[0].astype(BF16), h, post_mix_g[1], seq_len)
    h = ffn(h, 1)
    return h.reshape(batch, seq_len, d)
```
